```python
import jax, jax.numpy as jnp
from jax import lax
import numpy as np

D_MODEL = 1024
BATCH = 8
SEQ = 2048
DEPTH = 4

N_MIXERS = 2
N_DN_LAYERS = (DEPTH + 1) // 2
N_SB_LAYERS = DEPTH // 2
DN_HEADS = 8
DN_HEAD_DIM = D_MODEL // DN_HEADS
DN_CONV = 4
DN_CHUNK = 64
SB_HEADS = 16
SB_HEAD_DIM = D_MODEL // SB_HEADS
SB_BLOCK = 128
D_FF = -(-8 * D_MODEL // (3 * 256)) * 256
N_MOD = 6
EPS = 1e-6
ADA_INIT = 0.25

kernel_name = 'hybrid_deltanet_stickbreaking_adaln_trunk'


def rms_norm(x, g):
    xf = x.astype(jnp.float32)
    y = xf * lax.rsqrt(jnp.mean(xf * xf, axis=-1, keepdims=True) + EPS)
    return (y * g.astype(jnp.float32)).astype(x.dtype)


def l2_norm(x):
    return x * lax.rsqrt(jnp.sum(x * x, axis=-1, keepdims=True) + EPS)


def causal_dwconv(x, w):
    width = w.shape[0]
    return lax.conv_general_dilated(
        x, w[:, None, :].astype(x.dtype), window_strides=(1,), padding=[(width - 1, 0)],
        dimension_numbers=('NWC', 'WIO', 'NWC'), feature_group_count=x.shape[-1])


def chunk_gated_delta_rule(q, k, v, g, beta):
    B, T, H, dk = q.shape
    dv = v.shape[-1]
    C = DN_CHUNK
    N = T // C
    to_c = lambda t: t.transpose(0, 2, 1, 3).reshape(B, H, N, C, t.shape[-1])
    q = to_c(q) * (dk ** -0.5)
    k = to_c(k)
    v = to_c(v)
    g = g.transpose(0, 2, 1).reshape(B, H, N, C)
    beta = beta.transpose(0, 2, 1).reshape(B, H, N, C)
    G = jnp.cumsum(g, axis=-1)
    idx = jnp.arange(C)
    incl = idx[:, None] >= idx[None, :]
    strict = idx[:, None] > idx[None, :]
    decay = jnp.exp(jnp.where(incl, G[..., :, None] - G[..., None, :], -jnp.inf))
    k_beta = k * beta[..., None]
    v_beta = v * beta[..., None]
    A = jnp.where(strict, jnp.einsum('bhnid,bhnjd->bhnij', k_beta, k) * decay, 0.0)
    tri = jnp.eye(C, dtype=A.dtype) + A
    U = lax.linalg.triangular_solve(tri, v_beta, left_side=True, lower=True, unit_diagonal=True)
    W = lax.linalg.triangular_solve(tri, k_beta * jnp.exp(G)[..., None], left_side=True,
                                    lower=True, unit_diagonal=True)
    attn_intra = jnp.einsum('bhnid,bhnjd->bhnij', q, k) * decay
    q_dec = q * jnp.exp(G)[..., None]
    k_tail = k * jnp.exp(G[..., -1:] - G)[..., None]
    g_last = jnp.exp(G[..., -1])
    xs = tuple(jnp.moveaxis(t, 2, 0) for t in (q_dec, k_tail, U, W, attn_intra, g_last))

    def step(S, inp):
        qd, kt, u, w, a, gl = inp
        v_new = u - jnp.einsum('bhck,bhkv->bhcv', w, S)
        o = jnp.einsum('bhck,bhkv->bhcv', qd, S) + jnp.einsum('bhij,bhjv->bhiv', a, v_new)
        S = S * gl[..., None, None] + jnp.einsum('bhck,bhcv->bhkv', kt, v_new)
        return S, o

    S0 = jnp.zeros((B, H, dk, dv), jnp.float32)
    _, o = lax.scan(step, S0, xs)
    return jnp.moveaxis(o, 0, 2).reshape(B, H, T, dv).transpose(0, 2, 1, 3)


def gated_deltanet_mixer(h, w_in, conv_w, a_log, dt_bias, onorm_g, w_out):
    B, T, _ = h.shape
    H, d = DN_HEADS, DN_HEAD_DIM
    proj = h @ w_in
    qkv, z, a, b = jnp.split(proj, [3 * H * d, 4 * H * d, 4 * H * d + H], axis=-1)
    qkv = jax.nn.silu(causal_dwconv(qkv, conv_w)).astype(jnp.float32)
    q, k, v = [t.reshape(B, T, H, d) for t in jnp.split(qkv, 3, axis=-1)]
    q = l2_norm(q)
    k = l2_norm(k)
    beta = jax.nn.sigmoid(b.astype(jnp.float32))
    g = -jnp.exp(a_log.astype(jnp.float32)) * jax.nn.softplus(
        a.astype(jnp.float32) + dt_bias.astype(jnp.float32))
    o = chunk_gated_delta_rule(q, k, v, g, beta)
    o = rms_norm(o, onorm_g) * jax.nn.silu(z.reshape(B, T, H, d).astype(jnp.float32))
    return o.reshape(B, T, H * d).astype(h.dtype) @ w_out


def stick_breaking_mixer(h, w_qkv, q_norm_g, k_norm_g, w_out):
    B, T, _ = h.shape
    H, d = SB_HEADS, SB_HEAD_DIM
    qkv = (h @ w_qkv).reshape(B, T, 3, H, d)
    q = rms_norm(qkv[:, :, 0], q_norm_g).astype(jnp.float32).transpose(0, 2, 1, 3)
    k = rms_norm(qkv[:, :, 1], k_norm_g).astype(jnp.float32).transpose(0, 2, 1, 3)
    v = qkv[:, :, 2].astype(jnp.float32).transpose(0, 2, 1, 3)
    scale = d ** -0.5
    outs = []
    for blk in range(T // SB_BLOCK):
        q0 = blk * SB_BLOCK
        kv_len = q0 + SB_BLOCK
        z = jnp.einsum('bhqd,bhkd->bhqk', q[:, :, q0:kv_len], k[:, :, :kv_len]) * scale
        t_pos = q0 + jnp.arange(SB_BLOCK)
        s_pos = jnp.arange(kv_len)
        causal = s_pos[None, :] < t_pos[:, None]
        log_1m = jnp.where(causal, jax.nn.log_sigmoid(-z), 0.0)
        log_stick = lax.cumsum(log_1m, axis=3, reverse=True) - log_1m
        a = jnp.where(causal, jnp.exp(jax.nn.log_sigmoid(z) + log_stick), 0.0)
        outs.append(jnp.einsum('bhqk,bhkd->bhqd', a, v[:, :, :kv_len]))
    o = jnp.concatenate(outs, axis=2).transpose(0, 2, 1, 3).reshape(B, T, H * d)
    return o.astype(h.dtype) @ w_out


def swiglu(h, w_in, w_out):
    gate, up = jnp.split(h @ w_in, 2, axis=-1)
    return (jax.nn.silu(gate) * up) @ w_out


def setup_inputs(seed: int = 0) -> dict:
    key = jax.random.key(seed)
    ks = jax.random.split(key, 20)
    D, H, d = D_MODEL, DN_HEADS, DN_HEAD_DIM
    nrm = lambda k, shape, s: jax.random.normal(k, shape, jnp.float32) * s
    dn_in_cols = 4 * H * d + 2 * H
    dt = jnp.exp(jax.random.uniform(ks[8], (N_DN_LAYERS, H), jnp.float32, np.log(1e-3), np.log(1e-1)))
    return {
        'x': nrm(ks[0], (BATCH, SEQ, D), 1.0),
        'c': nrm(ks[1], (BATCH, D), 1.0),
        'ada_w': nrm(ks[2], (DEPTH, D, N_MOD * D), ADA_INIT * D ** -0.5),
        'ada_b': nrm(ks[3], (DEPTH, N_MOD * D), 0.01),
        'norm1_g': 1.0 + nrm(ks[4], (DEPTH, D), 0.02),
        'norm2_g': 1.0 + nrm(ks[5], (DEPTH, D), 0.02),
        'dn_w_in': nrm(ks[6], (N_DN_LAYERS, D, dn_in_cols), D ** -0.5),
        'dn_conv_w': nrm(ks[7], (N_DN_LAYERS, DN_CONV, 3 * H * d), DN_CONV ** -0.5),
        'dn_a_log': jnp.log(jax.random.uniform(ks[9], (N_DN_LAYERS, H), jnp.float32, 1.0, 16.0)),
        'dn_dt_bias': dt + jnp.log(-jnp.expm1(-dt)),
        'dn_onorm_g': 1.0 + nrm(ks[10], (N_DN_LAYERS, d), 0.02),
        'dn_w_out': nrm(ks[11], (N_DN_LAYERS, H * d, D), (H * d) ** -0.5),
        'sb_w_qkv': nrm(ks[12], (N_SB_LAYERS, D, 3 * SB_HEADS * SB_HEAD_DIM), D ** -0.5),
        'sb_q_norm_g': 1.0 + nrm(ks[13], (N_SB_LAYERS, SB_HEAD_DIM), 0.02),
        'sb_k_norm_g': 1.0 + nrm(ks[14], (N_SB_LAYERS, SB_HEAD_DIM), 0.02),
        'sb_w_out': nrm(ks[15], (N_SB_LAYERS, SB_HEADS * SB_HEAD_DIM, D), (SB_HEADS * SB_HEAD_DIM) ** -0.5),
        'ffn_w_in': nrm(ks[16], (DEPTH, D, 2 * D_FF), D ** -0.5),
        'ffn_w_out': nrm(ks[17], (DEPTH, D_FF, D), D_FF ** -0.5),
    }


def reference(x, c, ada_w, ada_b, norm1_g, norm2_g, dn_w_in, dn_conv_w, dn_a_log, dn_dt_bias,
              dn_onorm_g, dn_w_out, sb_w_qkv, sb_q_norm_g, sb_k_norm_g, sb_w_out, ffn_w_in, ffn_w_out):
    cond = jax.nn.silu(c)
    for i in range(DEPTH):
        mod = (cond @ ada_w[i] + ada_b[i])[:, None, :]
        sh1, sc1, gt1, sh2, sc2, gt2 = jnp.split(mod, N_MOD, axis=-1)
        h = rms_norm(x, norm1_g[i]) * (1.0 + sc1) + sh1
        j = i // N_MIXERS
        if i % N_MIXERS == 0:
            y = gated_deltanet_mixer(h, dn_w_in[j], dn_conv_w[j], dn_a_log[j], dn_dt_bias[j],
                                     dn_onorm_g[j], dn_w_out[j])
        else:
            y = stick_breaking_mixer(h, sb_w_qkv[j], sb_q_norm_g[j], sb_k_norm_g[j], sb_w_out[j])
        x = x + gt1 * y
        h = rms_norm(x, norm2_g[i]) * (1.0 + sc2) + sh2
        x = x + gt2 * swiglu(h, ffn_w_in[i], ffn_w_out[i])
    return x
```

```python
import functools

import jax
import jax.numpy as jnp
from jax import lax
from jax.experimental import pallas as pl
from jax.experimental.pallas import tpu as pltpu

F32 = jnp.float32
BF16 = jnp.bfloat16

EPS = 1e-6
N_MOD = 6
DN_HEADS = 8
DN_HEAD_DIM = 128
DN_CONV = 4
DN_CHUNK = 64
SB_HEADS = 16
SB_HEAD_DIM = 64
SB_BLOCK = 128
LANES = 128
VMEM_LIMIT_BYTES = 56 * 1024 * 1024


def _params(*semantics):
    return pltpu.CompilerParams(dimension_semantics=semantics, vmem_limit_bytes=VMEM_LIMIT_BYTES)


def _silu(x):
    return x / (1.0 + jnp.exp(-x))


def _softplus(x):
    return jnp.maximum(x, 0.0) + jnp.log(1.0 + jnp.exp(-jnp.abs(x)))


def _split_bf16(x):
    hi = x.astype(BF16)
    lo = (x - hi.astype(F32)).astype(BF16)
    return hi, lo


def _dot(a, b):
    return jnp.dot(a, b, preferred_element_type=F32)


def _dot_nt(a, b):
    return lax.dot_general(a, b, (((1,), (1,)), ((), ())), preferred_element_type=F32)


def _dot_tn(a, b):
    return lax.dot_general(a, b, (((0,), (0,)), ((), ())), preferred_element_type=F32)


def _adaln_kernel(c_ref, w_ref, b_ref, o_ref):
    cond = _silu(c_ref[...]).astype(BF16)
    o_ref[0] = _dot(cond, w_ref[0].astype(BF16)) + b_ref[0]


def adaln_modulation(c, ada_w, ada_b, *, tn=2048):
    depth, d, n = ada_w.shape
    b = c.shape[0]
    assert n % tn == 0
    return pl.pallas_call(
        _adaln_kernel,
        out_shape=jax.ShapeDtypeStruct((depth, b, n), F32),
        grid=(depth, n // tn),
        in_specs=[
            pl.BlockSpec((b, d), lambda l, j: (0, 0)),
            pl.BlockSpec((1, d, tn), lambda l, j: (l, 0, j)),
            pl.BlockSpec((1, 1, tn), lambda l, j: (l, 0, j)),
        ],
        out_specs=pl.BlockSpec((1, b, tn), lambda l, j: (l, 0, j)),
        compiler_params=_params("parallel", "parallel"),
        name="adaln_modulation",
    )(c, ada_w, ada_b.reshape(depth, 1, n))


def _norm_mod(x, g, sc, sh):
    ms = jnp.mean(x * x, axis=-1, keepdims=True)
    return (x * lax.rsqrt(ms + EPS) * g) * (1.0 + sc) + sh


def _nmm_kernel(x_ref, g_ref, sh_ref, sc_ref, w_ref, o_ref, h_scr):
    @pl.when(pl.program_id(1) == 0)
    def _():
        h_scr[...] = _norm_mod(x_ref[...], g_ref[...], sc_ref[0], sh_ref[0]).astype(BF16)

    o_ref[...] = _dot(h_scr[...], w_ref[...]).astype(o_ref.dtype)


def _nmm_swiglu_kernel(x_ref, g_ref, sh_ref, sc_ref, wg_ref, wu_ref, o_ref, h_scr):
    @pl.when(pl.program_id(1) == 0)
    def _():
        h_scr[...] = _norm_mod(x_ref[...], g_ref[...], sc_ref[0], sh_ref[0]).astype(BF16)

    h = h_scr[...]
    gate = _dot(h, wg_ref[...])
    up = _dot(h, wu_ref[...])
    o_ref[...] = (_silu(gate) * up).astype(o_ref.dtype)


def norm_mod_matmul(x2, g, mod3, layer, shift_idx, w, *, rows_per_batch, tm, tn, out_dtype, swiglu=False):
    m, d = x2.shape
    n = w.shape[1] // 2 if swiglu else w.shape[1]
    assert m % tm == 0 and n % tn == 0 and rows_per_batch % tm == 0
    nb = m // rows_per_batch
    blocks_per_batch = rows_per_batch // tm
    nj = n // tn

    def mod_map(chunk):
        return lambda i, j: (layer * nb + i // blocks_per_batch, 0, chunk)

    in_specs = [
        pl.BlockSpec((tm, d), lambda i, j: (i, 0)),
        pl.BlockSpec((1, d), lambda i, j: (0, 0)),
        pl.BlockSpec((1, 1, d), mod_map(shift_idx)),
        pl.BlockSpec((1, 1, d), mod_map(shift_idx + 1)),
        pl.BlockSpec((d, tn), lambda i, j: (0, j)),
    ]
    args = [x2, g.reshape(1, d), mod3, mod3, w]
    kernel = _nmm_kernel
    if swiglu:
        in_specs.append(pl.BlockSpec((d, tn), lambda i, j: (0, j + nj)))
        args.append(w)
        kernel = _nmm_swiglu_kernel
    return pl.pallas_call(
        kernel,
        out_shape=jax.ShapeDtypeStruct((m, n), out_dtype),
        grid=(m // tm, nj),
        in_specs=in_specs,
        out_specs=pl.BlockSpec((tm, tn), lambda i, j: (i, j)),
        scratch_shapes=[pltpu.VMEM((tm, d), BF16)],
        compiler_params=_params("parallel", "arbitrary"),
        name="norm_mod_swiglu" if swiglu else "norm_mod_matmul",
    )(*args)


def _mgr_kernel(a_ref, w_ref, x_ref, gt_ref, o_ref):
    o_ref[...] = x_ref[...] + gt_ref[0] * _dot(a_ref[...], w_ref[...])


def matmul_gate_residual(a, w, x2, mod3, layer, gate_idx, *, rows_per_batch, tm, tn):
    m, k = a.shape
    d = w.shape[1]
    assert m % tm == 0 and d % tn == 0 and rows_per_batch % tm == 0
    nb = m // rows_per_batch
    blocks_per_batch = rows_per_batch // tm
    return pl.pallas_call(
        _mgr_kernel,
        out_shape=jax.ShapeDtypeStruct((m, d), F32),
        grid=(m // tm, d // tn),
        in_specs=[
            pl.BlockSpec((tm, k), lambda i, j: (i, 0)),
            pl.BlockSpec((k, tn), lambda i, j: (0, j)),
            pl.BlockSpec((tm, tn), lambda i, j: (i, j)),
            pl.BlockSpec((1, 1, tn), lambda i, j: (layer * nb + i // blocks_per_batch, 0, gate_idx * (d // tn) + j)),
        ],
        out_specs=pl.BlockSpec((tm, tn), lambda i, j: (i, j)),
        compiler_params=_params("parallel", "parallel"),
        name="matmul_gate_residual",
    )(a, w, x2, mod3)


DN_GROUP = 4


def _dn_kernel(qp_ref, kp_ref, vp_ref, z_ref, ab_ref, cwq_ref, cwk_ref, cwv_ref, alog_ref, dtb_ref, og_ref,
               o_ref, qn_scr, kn_scr, vv_scr, gb_scr, bb_scr, lhs_scr, add_scr, oo_scr, s_scr):
    t = qp_ref.shape[1]
    c = DN_CHUNK
    d = DN_HEAD_DIM
    n_chunks = t // c
    h = pl.program_id(1)

    row = lax.broadcasted_iota(jnp.int32, (t, d), 0)

    def conv_silu(x, cw):
        acc = x * cw[DN_CONV - 1:DN_CONV, :]
        for s in range(1, DN_CONV):
            shifted = jnp.where(row >= s, pltpu.roll(x, s, 0), 0.0)
            acc = acc + shifted * cw[DN_CONV - 1 - s:DN_CONV - s, :]
        return _silu(acc)

    def l2n(x):
        return x * lax.rsqrt(jnp.sum(x * x, axis=-1, keepdims=True) + EPS)

    qn_scr[...] = l2n(conv_silu(qp_ref[0], cwq_ref[...])) * (d ** -0.5)
    kn_scr[...] = l2n(conv_silu(kp_ref[0], cwk_ref[...]))
    vv_scr[...] = conv_silu(vp_ref[0], cwv_ref[...])

    ab = ab_ref[0]
    lane = lax.broadcasted_iota(jnp.int32, (t, LANES), 1)
    g_all = -jnp.exp(alog_ref[...]) * _softplus(ab + dtb_ref[...])
    beta_all = 1.0 / (1.0 + jnp.exp(-ab))
    g_col = jnp.sum(jnp.where(lane == h, g_all, 0.0), axis=-1, keepdims=True)
    b_col = jnp.sum(jnp.where(lane == h + DN_HEADS, beta_all, 0.0), axis=-1, keepdims=True)
    gb_scr[...] = jnp.broadcast_to(g_col, (t, d))
    bb_scr[...] = jnp.broadcast_to(b_col, (t, d))

    ci = lax.broadcasted_iota(jnp.int32, (c, c), 0)
    cj = lax.broadcasted_iota(jnp.int32, (c, c), 1)
    incl = ci >= cj
    strict = ci > cj
    tri_incl = jnp.where(incl, 1.0, 0.0).astype(BF16)
    upper_incl = cj >= ci
    ones_cc = jnp.ones((c, c), BF16)
    eye_c = jnp.where(ci == cj, 1.0, 0.0)
    di = lax.broadcasted_iota(jnp.int32, (d, d), 0)
    dj = lax.broadcasted_iota(jnp.int32, (d, d), 1)
    eye_d = di == dj

    def prep_chunk(n):
        rows = pl.ds(pl.multiple_of(n * c, c), c)
        qn = qn_scr[rows, :]
        kn = kn_scr[rows, :]
        vv = vv_scr[rows, :]
        g_b = gb_scr[rows, :]
        beta_b = bb_scr[rows, :]

        g_hi, g_lo = _split_bf16(g_b)
        big_g = _dot(tri_incl, g_hi) + _dot(tri_incl, g_lo)
        gm = jnp.where(upper_incl, g_b[:, :c], 0.0)
        gm_hi, gm_lo = _split_bf16(gm)
        g_row = _dot(ones_cc, gm_hi) + _dot(ones_cc, gm_lo)
        decay = jnp.where(incl, jnp.exp(big_g[:, :c] - g_row), 0.0)

        kn16 = kn.astype(BF16)
        kk = _dot_nt(kn16, kn16)
        a_mat = jnp.where(strict, kk * beta_b[:, :c] * decay, 0.0)
        x_inv = eye_c - a_mat
        a16 = a_mat.astype(BF16)
        p = _dot(a16, a16)
        for it in range(5):
            p16 = p.astype(BF16)
            x_inv = x_inv + _dot(x_inv.astype(BF16), p16)
            if it < 4:
                p = _dot(p16, p16)

        exp_g = jnp.exp(big_g)
        rhs = jnp.concatenate([vv * beta_b, kn * beta_b * exp_g], axis=1).astype(BF16)
        uw = _dot(x_inv.astype(BF16), rhs).astype(BF16)
        attn = jnp.where(incl, _dot_nt(qn.astype(BF16), kn16) * decay, 0.0).astype(BF16)
        g_last = big_g[c - 1:c, :]
        k_tail = (kn * jnp.exp(g_last - big_g)).astype(BF16)
        kuw = _dot_tn(k_tail, uw)
        auw = _dot(attn, uw)
        m_mat = jnp.where(eye_d, jnp.exp(g_last), 0.0) - kuw[:, d:]
        q_eff = qn * exp_g - auw[:, d:]
        lhs_scr[n] = jnp.concatenate([m_mat, q_eff], axis=0).astype(BF16)
        add_scr[n] = jnp.concatenate([kuw[:, :d], auw[:, :d]], axis=0)

    def prep_group(gi, carry):
        for u in range(DN_GROUP):
            prep_chunk(gi * DN_GROUP + u)
        return carry

    lax.fori_loop(0, n_chunks // DN_GROUP, prep_group, 0)

    s_scr[...] = jnp.zeros((d, d), BF16)

    def step(n, carry):
        r = _dot(lhs_scr[n], s_scr[...]) + add_scr[n]
        oo_scr[pl.ds(pl.multiple_of(n * c, c), c), :] = r[d:, :]
        s_scr[...] = r[:d, :].astype(BF16)
        return carry

    lax.fori_loop(0, n_chunks, step, 0)

    o = oo_scr[...]
    on = o * lax.rsqrt(jnp.mean(o * o, axis=-1, keepdims=True) + EPS) * og_ref[...]
    o_ref[0] = (on * _silu(z_ref[0])).astype(o_ref.dtype)


def deltanet_core(proj, conv_w, a_log, dt_bias, onorm_g):
    b, t, _ = proj.shape
    hh, d, c = DN_HEADS, DN_HEAD_DIM, DN_CHUNK
    assert t % (c * DN_GROUP) == 0 and d == LANES and 2 * hh <= LANES
    n_chunks = t // c
    alog_pad = jnp.zeros((1, LANES), F32).at[0, :hh].set(a_log)
    dtb_pad = jnp.zeros((1, LANES), F32).at[0, :hh].set(dt_bias)

    def col(block):
        return pl.BlockSpec((1, t, d), lambda bi, hi: (bi, 0, block * hh + hi))

    def cw(block):
        return pl.BlockSpec((DN_CONV, d), lambda bi, hi: (0, block * hh + hi))

    vec = pl.BlockSpec((1, LANES), lambda bi, hi: (0, 0))
    return pl.pallas_call(
        _dn_kernel,
        out_shape=jax.ShapeDtypeStruct((b, t, hh * d), BF16),
        grid=(b, hh),
        in_specs=[col(0), col(1), col(2), col(3),
                  pl.BlockSpec((1, t, LANES), lambda bi, hi: (bi, 0, 4 * hh)),
                  cw(0), cw(1), cw(2), vec, vec, vec],
        out_specs=pl.BlockSpec((1, t, d), lambda bi, hi: (bi, 0, hi)),
        scratch_shapes=[
            pltpu.VMEM((t, d), F32), pltpu.VMEM((t, d), F32), pltpu.VMEM((t, d), F32),
            pltpu.VMEM((t, d), F32), pltpu.VMEM((t, d), F32),
            pltpu.VMEM((n_chunks, d + c, d), BF16), pltpu.VMEM((n_chunks, d + c, d), F32),
            pltpu.VMEM((t, d), F32), pltpu.VMEM((d, d), BF16),
        ],
        compiler_params=_params("parallel", "parallel"),
        name="deltanet_core",
    )(proj, proj, proj, proj, proj, conv_w, conv_w, conv_w, alog_pad, dtb_pad, onorm_g.reshape(1, d))


def _sb_kernel(q_ref, k_ref, v_ref, qg_ref, kg_ref, o_ref, kn_scr):
    blk = SB_BLOCK
    qi = pl.program_id(2)
    first = lax.broadcasted_iota(jnp.int32, (blk, LANES), 1) < SB_HEAD_DIM
    gi = lax.broadcasted_iota(jnp.int32, (LANES, LANES), 0) < SB_HEAD_DIM
    gj = lax.broadcasted_iota(jnp.int32, (LANES, LANES), 1) < SB_HEAD_DIM
    group_mean = jnp.where(gi == gj, 1.0 / SB_HEAD_DIM, 0.0).astype(BF16)

    def head_rms(x, g):
        sq_hi, sq_lo = _split_bf16(x * x)
        ms = _dot(sq_hi, group_mean) + _dot(sq_lo, group_mean)
        return x * lax.rsqrt(ms + EPS) * g

    @pl.when(qi == 0)
    def _():
        kn_scr[...] = head_rms(k_ref[0].astype(F32), kg_ref[...]).astype(BF16)

    qn = head_rms(q_ref[0].astype(F32), qg_ref[...]) * (SB_HEAD_DIM ** -0.5)
    q_heads = [jnp.where(first, qn, 0.0).astype(BF16), jnp.where(first, 0.0, qn).astype(BF16)]

    ri = lax.broadcasted_iota(jnp.int32, (blk, blk), 0)
    rj = lax.broadcasted_iota(jnp.int32, (blk, blk), 1)
    causal = rj < ri
    suffix = jnp.concatenate([jnp.where(ri > rj, 1.0, 0.0), jnp.ones((blk, blk), F32)], axis=1).astype(BF16)

    def block(j, q16, carry, acc, diagonal):
        rows = pl.ds(pl.multiple_of(j * blk, blk), blk)
        z = _dot_nt(q16, kn_scr[rows, :])
        sp = _softplus(z)
        log_1m = -sp
        if diagonal:
            log_1m = jnp.where(causal, log_1m, 0.0)
        hi, lo = _split_bf16(log_1m)
        cs = _dot(hi, suffix) + _dot(lo, suffix)
        log_a = z - sp + cs[:, :blk] + carry
        a = jnp.exp(log_a)
        if diagonal:
            a = jnp.where(causal, a, 0.0)
        acc = acc + _dot(a.astype(BF16), v_ref[0, rows, :])
        return carry + cs[:, blk:], acc

    zeros = jnp.zeros((blk, blk), F32)
    state = []
    for p in range(2):
        state.extend(block(qi, q_heads[p], zeros, zeros, True))

    def body(it, st):
        j = qi - 1 - it
        out = []
        for p in range(2):
            out.extend(block(j, q_heads[p], st[2 * p], st[2 * p + 1], False))
        return tuple(out)

    st = lax.fori_loop(0, qi, body, tuple(state))
    o_ref[0] = jnp.where(first, st[1], st[3]).astype(o_ref.dtype)


def stickbreak_core(qkv, q_norm_g, k_norm_g):
    b, t, _ = qkv.shape
    pairs = SB_HEADS * SB_HEAD_DIM // LANES
    assert 2 * SB_HEAD_DIM == LANES and t % SB_BLOCK == 0
    qg = jnp.tile(q_norm_g, LANES // SB_HEAD_DIM).reshape(1, LANES)
    kg = jnp.tile(k_norm_g, LANES // SB_HEAD_DIM).reshape(1, LANES)
    vec = pl.BlockSpec((1, LANES), lambda bi, hp, qi: (0, 0))
    return pl.pallas_call(
        _sb_kernel,
        out_shape=jax.ShapeDtypeStruct((b, t, pairs * LANES), BF16),
        grid=(b, pairs, t // SB_BLOCK),
        in_specs=[
            pl.BlockSpec((1, SB_BLOCK, LANES), lambda bi, hp, qi: (bi, qi, hp)),
            pl.BlockSpec((1, t, LANES), lambda bi, hp, qi: (bi, 0, pairs + hp)),
            pl.BlockSpec((1, t, LANES), lambda bi, hp, qi: (bi, 0, 2 * pairs + hp)),
            vec, vec,
        ],
        out_specs=pl.BlockSpec((1, SB_BLOCK, LANES), lambda bi, hp, qi: (bi, qi, hp)),
        scratch_shapes=[pltpu.VMEM((t, LANES), BF16)],
        compiler_params=_params("parallel", "parallel", "arbitrary"),
        name="stickbreak_core",
    )(qkv, qkv, qkv, qg, kg)


def _pick(n, candidates):
    for cand in candidates:
        if n % cand == 0:
            return cand
    raise ValueError(f"no tile for {n}")


def kernel(x, c, ada_w, ada_b, norm1_g, norm2_g, dn_w_in, dn_conv_w, dn_a_log, dn_dt_bias, dn_onorm_g, dn_w_out,
           sb_w_qkv, sb_q_norm_g, sb_k_norm_g, sb_w_out, ffn_w_in, ffn_w_out):
    b, t, d = x.shape
    depth = ada_w.shape[0]
    m = b * t
    tm = _pick(t, (1024, 512, 256, 128))
    kw = dict(rows_per_batch=t, tm=tm)

    mod = adaln_modulation(c, ada_w, ada_b, tn=_pick(N_MOD * d, (2048, 1024, 512, 128)))
    mod3 = mod.reshape(depth * b, 1, N_MOD * d)

    dn_cols = dn_w_in.shape[-1]
    dn_pad = 4 * DN_HEADS * DN_HEAD_DIM + LANES - dn_cols
    dn_w_in16 = jnp.pad(dn_w_in, ((0, 0), (0, 0), (0, dn_pad))).astype(BF16)
    dn_w_out16 = dn_w_out.astype(BF16)
    sb_w_qkv16 = sb_w_qkv.astype(BF16)
    sb_w_out16 = sb_w_out.astype(BF16)
    ffn_w_in16 = ffn_w_in.astype(BF16)
    ffn_w_out16 = ffn_w_out.astype(BF16)
    d_ff = ffn_w_out.shape[1]

    x2 = x.reshape(m, d)
    for i in range(depth):
        j = i // 2
        if i % 2 == 0:
            n_in = dn_w_in16.shape[-1]
            proj = norm_mod_matmul(x2, norm1_g[i], mod3, i, 0, dn_w_in16[j], out_dtype=F32,
                                   tn=_pick(n_in, (1408, 1024, 768, 384, 128)), **kw)
            y = deltanet_core(proj.reshape(b, t, n_in), dn_conv_w[j], dn_a_log[j], dn_dt_bias[j], dn_onorm_g[j])
            w_out = dn_w_out16[j]
        else:
            n_in = sb_w_qkv16.shape[-1]
            qkv = norm_mod_matmul(x2, norm1_g[i], mod3, i, 0, sb_w_qkv16[j], out_dtype=BF16,
                                  tn=_pick(n_in, (1024, 768, 512, 384, 128)), **kw)
            y = stickbreak_core(qkv.reshape(b, t, n_in), sb_q_norm_g[j], sb_k_norm_g[j])
            w_out = sb_w_out16[j]
        x2 = matmul_gate_residual(y.reshape(m, d), w_out, x2, mod3, i, 2, tn=_pick(d, (512, 256, 128)), **kw)
        act = norm_mod_matmul(x2, norm2_g[i], mod3, i, 3, ffn_w_in16[i], out_dtype=BF16, swiglu=True,
                              tn=_pick(d_ff, (1408, 1024, 512, 256, 128)), **kw)
        x2 = matmul_gate_residual(act, ffn_w_out16[i], x2, mod3, i, 5, tn=_pick(d, (512, 256, 128)), **kw)
    return x2.reshape(b, t, d)
```

```python
import functools

import jax
import jax.numpy as jnp
from jax import lax
from jax.experimental import pallas as pl
from jax.experimental.pallas import tpu as pltpu

F32 = jnp.float32
BF16 = jnp.bfloat16

EPS = 1e-6
N_MOD = 6
DN_HEADS = 8
DN_HEAD_DIM = 128
DN_CONV = 4
DN_CHUNK = 128
SB_HEADS = 16
SB_HEAD_DIM = 64
SB_BLOCK = 128
SB_QTILE = 1024
LOG2_E = 1.4426950408889634
LANES = 128
VMEM_LIMIT_BYTES = 56 * 1024 * 1024


def _params(*semantics):
    return pltpu.CompilerParams(dimension_semantics=semantics, vmem_limit_bytes=VMEM_LIMIT_BYTES)


def _silu(x):
    return x / (1.0 + jnp.exp(-x))


def _softplus(x):
    return jnp.maximum(x, 0.0) + jnp.log(1.0 + jnp.exp(-jnp.abs(x)))


def _split_bf16(x):
    hi = x.astype(BF16)
    lo = (x - hi.astype(F32)).astype(BF16)
    return hi, lo


def _dot(a, b):
    return jnp.dot(a, b, preferred_element_type=F32)


def _dot_nt(a, b):
    return lax.dot_general(a, b, (((1,), (1,)), ((), ())), preferred_element_type=F32)


def _dot_tn(a, b):
    return lax.dot_general(a, b, (((0,), (0,)), ((), ())), preferred_element_type=F32)


def _adaln_kernel(c_ref, w_ref, b_ref, o_ref):
    cond = _silu(c_ref[...]).astype(BF16)
    o_ref[0] = _dot(cond, w_ref[0].astype(BF16)) + b_ref[0]


def adaln_modulation(c, ada_w, ada_b, *, tn=2048):
    depth, d, n = ada_w.shape
    b = c.shape[0]
    assert n % tn == 0
    return pl.pallas_call(
        _adaln_kernel,
        out_shape=jax.ShapeDtypeStruct((depth, b, n), F32),
        grid=(depth, n // tn),
        in_specs=[
            pl.BlockSpec((b, d), lambda l, j: (0, 0)),
            pl.BlockSpec((1, d, tn), lambda l, j: (l, 0, j)),
            pl.BlockSpec((1, 1, tn), lambda l, j: (l, 0, j)),
        ],
        out_specs=pl.BlockSpec((1, b, tn), lambda l, j: (l, 0, j)),
        compiler_params=_params("parallel", "parallel"),
        name="adaln_modulation",
    )(c, ada_w, ada_b.reshape(depth, 1, n))


def _norm_mod(x, g, sc, sh):
    ms = jnp.mean(x * x, axis=-1, keepdims=True)
    return (x * lax.rsqrt(ms + EPS) * g) * (1.0 + sc) + sh


def _nmm_kernel(x_ref, g_ref, sh_ref, sc_ref, w_ref, o_ref, h_scr):
    @pl.when(pl.program_id(1) == 0)
    def _():
        h_scr[...] = _norm_mod(x_ref[...], g_ref[...], sc_ref[0], sh_ref[0]).astype(BF16)

    o_ref[...] = _dot(h_scr[...], w_ref[...]).astype(o_ref.dtype)


def _nmm_swiglu_kernel(x_ref, g_ref, sh_ref, sc_ref, wg_ref, wu_ref, o_ref, h_scr):
    @pl.when(pl.program_id(1) == 0)
    def _():
        h_scr[...] = _norm_mod(x_ref[...], g_ref[...], sc_ref[0], sh_ref[0]).astype(BF16)

    h = h_scr[...]
    gate = _dot(h, wg_ref[...])
    up = _dot(h, wu_ref[...])
    o_ref[...] = (_silu(gate) * up).astype(o_ref.dtype)


def norm_mod_matmul(x2, g, mod3, layer, shift_idx, w, *, rows_per_batch, tm, tn, out_dtype, swiglu=False):
    m, d = x2.shape
    n = w.shape[1] // 2 if swiglu else w.shape[1]
    assert m % tm == 0 and n % tn == 0 and rows_per_batch % tm == 0
    nb = m // rows_per_batch
    blocks_per_batch = rows_per_batch // tm
    nj = n // tn

    def mod_map(chunk):
        return lambda i, j: (layer * nb + i // blocks_per_batch, 0, chunk)

    in_specs = [
        pl.BlockSpec((tm, d), lambda i, j: (i, 0)),
        pl.BlockSpec((1, d), lambda i, j: (0, 0)),
        pl.BlockSpec((1, 1, d), mod_map(shift_idx)),
        pl.BlockSpec((1, 1, d), mod_map(shift_idx + 1)),
        pl.BlockSpec((d, tn), lambda i, j: (0, j)),
    ]
    args = [x2, g.reshape(1, d), mod3, mod3, w]
    kernel = _nmm_kernel
    if swiglu:
        in_specs.append(pl.BlockSpec((d, tn), lambda i, j: (0, j + nj)))
        args.append(w)
        kernel = _nmm_swiglu_kernel
    return pl.pallas_call(
        kernel,
        out_shape=jax.ShapeDtypeStruct((m, n), out_dtype),
        grid=(m // tm, nj),
        in_specs=in_specs,
        out_specs=pl.BlockSpec((tm, tn), lambda i, j: (i, j)),
        scratch_shapes=[pltpu.VMEM((tm, d), BF16)],
        compiler_params=_params("parallel", "arbitrary"),
        name="norm_mod_swiglu" if swiglu else "norm_mod_matmul",
    )(*args)


def _mgr_kernel(a_ref, w_ref, x_ref, gt_ref, o_ref):
    o_ref[...] = x_ref[...] + gt_ref[0] * _dot(a_ref[...], w_ref[...])


def matmul_gate_residual(a, w, x2, mod3, layer, gate_idx, *, rows_per_batch, tm, tn):
    m, k = a.shape
    d = w.shape[1]
    assert m % tm == 0 and d % tn == 0 and rows_per_batch % tm == 0
    nb = m // rows_per_batch
    blocks_per_batch = rows_per_batch // tm
    return pl.pallas_call(
        _mgr_kernel,
        out_shape=jax.ShapeDtypeStruct((m, d), F32),
        grid=(m // tm, d // tn),
        in_specs=[
            pl.BlockSpec((tm, k), lambda i, j: (i, 0)),
            pl.BlockSpec((k, tn), lambda i, j: (0, j)),
            pl.BlockSpec((tm, tn), lambda i, j: (i, j)),
            pl.BlockSpec((1, 1, tn), lambda i, j: (layer * nb + i // blocks_per_batch, 0, gate_idx * (d // tn) + j)),
        ],
        out_specs=pl.BlockSpec((tm, tn), lambda i, j: (i, j)),
        compiler_params=_params("parallel", "parallel"),
        name="matmul_gate_residual",
    )(a, w, x2, mod3)


DN_GROUP = 4


def _dn_kernel(qp_ref, kp_ref, vp_ref, z_ref, ab_ref, cwq_ref, cwk_ref, cwv_ref, alog_ref, dtb_ref, og_ref,
               o_ref, qn_scr, kn_scr, vv_scr, gb_scr, bb_scr, lhs_scr, add_scr, oo_scr, s_scr):
    t = qp_ref.shape[1]
    c = DN_CHUNK
    d = DN_HEAD_DIM
    n_chunks = t // c
    h = pl.program_id(1)

    row = lax.broadcasted_iota(jnp.int32, (t, d), 0)

    def conv_silu(x, cw):
        acc = x * cw[DN_CONV - 1:DN_CONV, :]
        for s in range(1, DN_CONV):
            shifted = jnp.where(row >= s, pltpu.roll(x, s, 0), 0.0)
            acc = acc + shifted * cw[DN_CONV - 1 - s:DN_CONV - s, :]
        return _silu(acc)

    def l2n(x):
        return x * lax.rsqrt(jnp.sum(x * x, axis=-1, keepdims=True) + EPS)

    qn_scr[...] = l2n(conv_silu(qp_ref[0], cwq_ref[...])) * (d ** -0.5)
    kn_scr[...] = l2n(conv_silu(kp_ref[0], cwk_ref[...]))
    vv_scr[...] = conv_silu(vp_ref[0], cwv_ref[...])

    ab = ab_ref[0]
    lane = lax.broadcasted_iota(jnp.int32, (t, LANES), 1)
    g_all = -jnp.exp(alog_ref[...]) * _softplus(ab + dtb_ref[...])
    beta_all = 1.0 / (1.0 + jnp.exp(-ab))
    g_col = jnp.sum(jnp.where(lane == h, g_all, 0.0), axis=-1, keepdims=True)
    b_col = jnp.sum(jnp.where(lane == h + DN_HEADS, beta_all, 0.0), axis=-1, keepdims=True)
    gb_scr[...] = jnp.broadcast_to(g_col, (t, d))
    bb_scr[...] = jnp.broadcast_to(b_col, (t, d))

    ci = lax.broadcasted_iota(jnp.int32, (c, c), 0)
    cj = lax.broadcasted_iota(jnp.int32, (c, c), 1)
    incl = ci >= cj
    strict = ci > cj
    upper_incl = cj >= ci
    eye = ci == cj
    eye_f = jnp.where(eye, 1.0, 0.0)
    tri_ones = jnp.concatenate([jnp.where(incl, 1.0, 0.0), jnp.ones((c, c), F32)], axis=1).astype(BF16)
    tri_ones2 = jnp.concatenate([tri_ones, tri_ones], axis=1)
    zeros_cc = jnp.zeros((c, c), F32)
    half = c // 2
    half_strict = strict & ((ci < half) == (cj < half))
    lower_left = (ci >= half) & (cj < half)
    n_factors = half.bit_length() - 2

    def prep_group(gi, carry):
        grp = range(DN_GROUP)
        ns = [gi * DN_GROUP + u for u in grp]
        rows = [pl.ds(pl.multiple_of(n * c, c), c) for n in ns]
        qn = [qn_scr[r, :] for r in rows]
        kn = [kn_scr[r, :] for r in rows]
        vv = [vv_scr[r, :] for r in rows]
        g_b = [gb_scr[r, :] for r in rows]
        beta_b = [bb_scr[r, :] for r in rows]

        def cum(g):
            top = jnp.concatenate([g, g], axis=1)
            bot = jnp.concatenate([zeros_cc, -jnp.where(upper_incl, g, 0.0)], axis=1)
            hi, lo = _split_bf16(jnp.concatenate([top, bot], axis=0))
            return _dot(tri_ones2, jnp.concatenate([hi, lo], axis=0))

        gd = [cum(g) for g in g_b]
        big_g = [x[:, :c] for x in gd]
        decay = [jnp.where(incl, jnp.exp(x[:, c:]), 0.0) for x in gd]
        kn16 = [k.astype(BF16) for k in kn]
        qk_kk = [_dot_nt(jnp.concatenate([q.astype(BF16), k16], axis=0), k16) for q, k16 in zip(qn, kn16)]
        attn = [jnp.where(incl, r[:c] * dc, 0.0).astype(BF16) for r, dc in zip(qk_kk, decay)]
        a_mat = [r[c:] * bb * dc for r, bb, dc in zip(qk_kk, beta_b, decay)]

        a_diag = [jnp.where(half_strict, a, 0.0) for a in a_mat]
        a_off = [jnp.where(lower_left, a, 0.0).astype(BF16) for a in a_mat]
        x_inv = [eye_f - a for a in a_diag]
        a16 = [a.astype(BF16) for a in a_diag]
        p16 = [_dot(a, a).astype(BF16) for a in a16]
        for _ in range(n_factors - 1):
            r = [_dot(jnp.concatenate([p, x.astype(BF16)], axis=0), p) for p, x in zip(p16, x_inv)]
            x_inv = [x + y[c:] for x, y in zip(x_inv, r)]
            p16 = [y[:c].astype(BF16) for y in r]
        x_inv = [x + _dot(x.astype(BF16), p) for x, p in zip(x_inv, p16)]
        x16 = [x.astype(BF16) for x in x_inv]
        off = [_dot(a, x).astype(BF16) for a, x in zip(a_off, x16)]
        x_inv = [x - _dot(x_b, y) for x, x_b, y in zip(x_inv, x16, off)]

        exp_g = [jnp.exp(g) for g in big_g]
        rhs = [jnp.concatenate([v * bb, k * bb * eg], axis=1).astype(BF16)
               for v, k, bb, eg in zip(vv, kn, beta_b, exp_g)]
        uw = [_dot(x.astype(BF16), r).astype(BF16) for x, r in zip(x_inv, rhs)]
        g_last = [g[c - 1:c, :] for g in big_g]
        k_tail = [(k * jnp.exp(gl - g)).astype(BF16) for k, gl, g in zip(kn, g_last, big_g)]
        kuw = [_dot_tn(kt, y) for kt, y in zip(k_tail, uw)]
        auw = [_dot(at, y) for at, y in zip(attn, uw)]
        for u in grp:
            m_mat = jnp.where(eye, jnp.exp(g_last[u]), 0.0) - kuw[u][:, d:]
            q_eff = qn[u] * exp_g[u] - auw[u][:, d:]
            lhs_scr[ns[u]] = jnp.concatenate([m_mat, q_eff], axis=0).astype(BF16)
            add_scr[ns[u]] = jnp.concatenate([kuw[u][:, :d], auw[u][:, :d]], axis=0)
        return carry

    lax.fori_loop(0, n_chunks // DN_GROUP, prep_group, 0)

    s_scr[...] = jnp.zeros((d, d), BF16)

    def step(n, carry):
        r = _dot(lhs_scr[n], s_scr[...]) + add_scr[n]
        oo_scr[pl.ds(pl.multiple_of(n * c, c), c), :] = r[d:, :]
        s_scr[...] = r[:d, :].astype(BF16)
        return carry

    lax.fori_loop(0, n_chunks, step, 0)

    o = oo_scr[...]
    on = o * lax.rsqrt(jnp.mean(o * o, axis=-1, keepdims=True) + EPS) * og_ref[...]
    o_ref[0] = (on * _silu(z_ref[0])).astype(o_ref.dtype)


def deltanet_core(proj, conv_w, a_log, dt_bias, onorm_g):
    b, t, _ = proj.shape
    hh, d, c = DN_HEADS, DN_HEAD_DIM, DN_CHUNK
    assert t % (c * DN_GROUP) == 0 and c == d == LANES and 2 * hh <= LANES
    n_chunks = t // c
    alog_pad = jnp.zeros((1, LANES), F32).at[0, :hh].set(a_log)
    dtb_pad = jnp.zeros((1, LANES), F32).at[0, :hh].set(dt_bias)

    def col(block):
        return pl.BlockSpec((1, t, d), lambda bi, hi: (bi, 0, block * hh + hi))

    def cw(block):
        return pl.BlockSpec((DN_CONV, d), lambda bi, hi: (0, block * hh + hi))

    vec = pl.BlockSpec((1, LANES), lambda bi, hi: (0, 0))
    return pl.pallas_call(
        _dn_kernel,
        out_shape=jax.ShapeDtypeStruct((b, t, hh * d), BF16),
        grid=(b, hh),
        in_specs=[col(0), col(1), col(2), col(3),
                  pl.BlockSpec((1, t, LANES), lambda bi, hi: (bi, 0, 4 * hh)),
                  cw(0), cw(1), cw(2), vec, vec, vec],
        out_specs=pl.BlockSpec((1, t, d), lambda bi, hi: (bi, 0, hi)),
        scratch_shapes=[
            pltpu.VMEM((t, d), F32), pltpu.VMEM((t, d), F32), pltpu.VMEM((t, d), F32),
            pltpu.VMEM((t, d), F32), pltpu.VMEM((t, d), F32),
            pltpu.VMEM((n_chunks, d + c, d), BF16), pltpu.VMEM((n_chunks, d + c, d), F32),
            pltpu.VMEM((t, d), F32), pltpu.VMEM((d, d), BF16),
        ],
        compiler_params=_params("parallel", "parallel"),
        name="deltanet_core",
    )(proj, proj, proj, proj, proj, conv_w, conv_w, conv_w, alog_pad, dtb_pad, onorm_g.reshape(1, d))


def _sb_kernel(q_ref, k_ref, v_ref, qg_ref, kg_ref, o_ref, kn_scr, vm_scr, acc_scr, carry_scr):
    blk = SB_BLOCK
    qt_rows = q_ref.shape[1]
    sub_blocks = qt_rows // blk
    qt = pl.program_id(2)
    first = lax.broadcasted_iota(jnp.int32, (1, LANES), 1) < SB_HEAD_DIM
    gi = lax.broadcasted_iota(jnp.int32, (LANES, LANES), 0) < SB_HEAD_DIM
    gj = lax.broadcasted_iota(jnp.int32, (LANES, LANES), 1) < SB_HEAD_DIM
    group_mean1 = jnp.where(gi == gj, 1.0 / SB_HEAD_DIM, 0.0)
    group_mean = jnp.concatenate([group_mean1, group_mean1], axis=0).astype(BF16)

    def head_rms(x, g):
        sq_hi, sq_lo = _split_bf16(x * x)
        ms = _dot(jnp.concatenate([sq_hi, sq_lo], axis=1), group_mean)
        return x * lax.rsqrt(ms + EPS) * g

    @pl.when(qt == 0)
    def _():
        kn_scr[...] = head_rms(k_ref[0].astype(F32), kg_ref[...]).astype(BF16)
        v = v_ref[0]
        vm_scr[0] = jnp.where(first, v, jnp.zeros_like(v))
        vm_scr[1] = jnp.where(first, jnp.zeros_like(v), v)

    qn = head_rms(q_ref[0].astype(F32), qg_ref[...]) * (SB_HEAD_DIM ** -0.5 * LOG2_E)
    q_heads = [jnp.where(first, qn, 0.0).astype(BF16), jnp.where(first, 0.0, qn).astype(BF16)]

    ri = lax.broadcasted_iota(jnp.int32, (blk, blk), 0)
    rj = lax.broadcasted_iota(jnp.int32, (blk, blk), 1)
    suffix1 = jnp.concatenate([jnp.where(ri > rj, 1.0, 0.0), jnp.ones((blk, blk), F32)], axis=1)
    suffix = jnp.concatenate([suffix1, suffix1], axis=0).astype(BF16)

    acc_scr[...] = jnp.zeros_like(acc_scr)
    carry_scr[...] = jnp.zeros_like(carry_scr)

    def sweep(j, r0, diagonal):
        krows = pl.ds(pl.multiple_of(j * blk, blk), blk)
        n_rows = qt_rows - r0
        if diagonal:
            causal = (lax.broadcasted_iota(jnp.int32, (n_rows, blk), 1)
                      < lax.broadcasted_iota(jnp.int32, (n_rows, blk), 0))
        pv = []
        for p in range(2):
            z = _dot_nt(q_heads[p][r0:, :], kn_scr[krows, :])
            neg_abs = pltpu.bitcast(pltpu.bitcast(z, jnp.uint32) | jnp.uint32(0x80000000), F32)
            sp = jnp.maximum(z, 0.0) + jnp.log2(1.0 + jnp.exp2(neg_abs))
            sp_m = jnp.where(causal, sp, 0.0) if diagonal else sp
            hi, lo = _split_bf16(sp_m)
            cs = _dot(jnp.concatenate([hi, lo], axis=1), suffix)
            a = jnp.exp2(z - sp - cs[:, :blk] - carry_scr[p, r0:, :])
            if diagonal:
                a = jnp.where(causal, a, 0.0)
            pv.append(_dot(a.astype(BF16), vm_scr[p, krows, :]))
            carry_scr[p, r0:, :] += cs[:, blk:]
        acc_scr[r0:, :] += pv[0] + pv[1]

    base = qt * sub_blocks
    for jj in reversed(range(sub_blocks)):
        sweep(base + jj, jj * blk, True)

    def body(it, carry):
        sweep(base - 1 - it, 0, False)
        return carry

    lax.fori_loop(0, base, body, 0)
    o_ref[0] = acc_scr[...].astype(o_ref.dtype)


def stickbreak_core(qkv, q_norm_g, k_norm_g):
    b, t, _ = qkv.shape
    pairs = SB_HEADS * SB_HEAD_DIM // LANES
    qt_rows = min(SB_QTILE, t)
    assert 2 * SB_HEAD_DIM == LANES and t % qt_rows == 0 and qt_rows % SB_BLOCK == 0
    qg = jnp.tile(q_norm_g, LANES // SB_HEAD_DIM).reshape(1, LANES)
    kg = jnp.tile(k_norm_g, LANES // SB_HEAD_DIM).reshape(1, LANES)
    vec = pl.BlockSpec((1, LANES), lambda bi, hp, qi: (0, 0))
    return pl.pallas_call(
        _sb_kernel,
        out_shape=jax.ShapeDtypeStruct((b, t, pairs * LANES), BF16),
        grid=(b, pairs, t // qt_rows),
        in_specs=[
            pl.BlockSpec((1, qt_rows, LANES), lambda bi, hp, qi: (bi, qi, hp)),
            pl.BlockSpec((1, t, LANES), lambda bi, hp, qi: (bi, 0, pairs + hp)),
            pl.BlockSpec((1, t, LANES), lambda bi, hp, qi: (bi, 0, 2 * pairs + hp)),
            vec, vec,
        ],
        out_specs=pl.BlockSpec((1, qt_rows, LANES), lambda bi, hp, qi: (bi, qi, hp)),
        scratch_shapes=[
            pltpu.VMEM((t, LANES), BF16), pltpu.VMEM((2, t, LANES), BF16),
            pltpu.VMEM((qt_rows, LANES), F32), pltpu.VMEM((2, qt_rows, LANES), F32),
        ],
        compiler_params=_params("parallel", "parallel", "arbitrary"),
        name="stickbreak_core",
    )(qkv, qkv, qkv, qg, kg)


def _pick(n, candidates):
    for cand in candidates:
        if n % cand == 0:
            return cand
    raise ValueError(f"no tile for {n}")


def kernel(x, c, ada_w, ada_b, norm1_g, norm2_g, dn_w_in, dn_conv_w, dn_a_log, dn_dt_bias, dn_onorm_g, dn_w_out,
           sb_w_qkv, sb_q_norm_g, sb_k_norm_g, sb_w_out, ffn_w_in, ffn_w_out):
    b, t, d = x.shape
    depth = ada_w.shape[0]
    m = b * t
    tm = _pick(t, (1024, 512, 256, 128))
    kw = dict(rows_per_batch=t, tm=tm)

    mod = adaln_modulation(c, ada_w, ada_b, tn=_pick(N_MOD * d, (2048, 1024, 512, 128)))
    mod3 = mod.reshape(depth * b, 1, N_MOD * d)

    dn_cols = dn_w_in.shape[-1]
    dn_pad = 4 * DN_HEADS * DN_HEAD_DIM + LANES - dn_cols
    dn_w_in16 = jnp.pad(dn_w_in, ((0, 0), (0, 0), (0, dn_pad))).astype(BF16)
    dn_w_out16 = dn_w_out.astype(BF16)
    sb_w_qkv16 = sb_w_qkv.astype(BF16)
    sb_w_out16 = sb_w_out.astype(BF16)
    ffn_w_in16 = ffn_w_in.astype(BF16)
    ffn_w_out16 = ffn_w_out.astype(BF16)
    d_ff = ffn_w_out.shape[1]

    x2 = x.reshape(m, d)
    for i in range(depth):
        j = i // 2
        if i % 2 == 0:
            n_in = dn_w_in16.shape[-1]
            proj = norm_mod_matmul(x2, norm1_g[i], mod3, i, 0, dn_w_in16[j], out_dtype=F32,
                                   tn=_pick(n_in, (1408, 1024, 768, 384, 128)), **kw)
            y = deltanet_core(proj.reshape(b, t, n_in), dn_conv_w[j], dn_a_log[j], dn_dt_bias[j], dn_onorm_g[j])
            w_out = dn_w_out16[j]
        else:
            n_in = sb_w_qkv16.shape[-1]
            qkv = norm_mod_matmul(x2, norm1_g[i], mod3, i, 0, sb_w_qkv16[j], out_dtype=BF16,
                                  tn=_pick(n_in, (1024, 768, 512, 384, 128)), **kw)
            y = stickbreak_core(qkv.reshape(b, t, n_in), sb_q_norm_g[j], sb_k_norm_g[j])
            w_out = sb_w_out16[j]
        x2 = matmul_gate_residual(y.reshape(m, d), w_out, x2, mod3, i, 2, tn=_pick(d, (512, 256, 128)), **kw)
        act = norm_mod_matmul(x2, norm2_g[i], mod3, i, 3, ffn_w_in16[i], out_dtype=BF16, swiglu=True,
                              tn=_pick(d_ff, (1408, 1024, 512, 256, 128)), **kw)
        x2 = matmul_gate_residual(act, ffn_w_out16[i], x2, mod3, i, 5, tn=_pick(d, (512, 256, 128)), **kw)
    return x2.reshape(b, t, d)
```

```python
import jax
import jax.numpy as jnp
from jax import lax
from jax.experimental import pallas as pl
from jax.experimental.pallas import tpu as pltpu

F32 = jnp.float32
BF16 = jnp.bfloat16

EPS = 1e-6
N_MOD = 6
DN_HEADS = 8
DN_HEAD_DIM = 128
DN_CONV = 4
DN_CHUNK = 128
SB_HEADS = 16
SB_HEAD_DIM = 64
SB_BLOCK = 128
SB_QTILE = 1024
SB_BAND = 3
SB_ZERO_LOG2 = 151.0
LOG2_E = 1.4426950408889634
LANES = 128
VMEM_LIMIT_BYTES = 56 * 1024 * 1024


def _params(*semantics):
    return pltpu.CompilerParams(dimension_semantics=semantics, vmem_limit_bytes=VMEM_LIMIT_BYTES)


def _silu(x):
    return x / (1.0 + jnp.exp(-x))


def _softplus(x):
    return jnp.maximum(x, 0.0) + jnp.log(1.0 + jnp.exp(-jnp.abs(x)))


def _split_bf16(x):
    hi = x.astype(BF16)
    lo = (x - hi.astype(F32)).astype(BF16)
    return hi, lo


def _dot(a, b):
    return jnp.dot(a, b, preferred_element_type=F32)


def _dot_nt(a, b):
    return lax.dot_general(a, b, (((1,), (1,)), ((), ())), preferred_element_type=F32)


def _dot_tn(a, b):
    return lax.dot_general(a, b, (((0,), (0,)), ((), ())), preferred_element_type=F32)


def _adaln_kernel(c_ref, w_ref, b_ref, o_ref):
    cond = _silu(c_ref[...]).astype(BF16)
    o_ref[0] = _dot(cond, w_ref[0].astype(BF16)) + b_ref[0]


def adaln_modulation(c, ada_w, ada_b, *, tn=2048):
    depth, d, n = ada_w.shape
    b = c.shape[0]
    assert n % tn == 0
    return pl.pallas_call(
        _adaln_kernel,
        out_shape=jax.ShapeDtypeStruct((depth, b, n), F32),
        grid=(depth, n // tn),
        in_specs=[
            pl.BlockSpec((b, d), lambda l, j: (0, 0)),
            pl.BlockSpec((1, d, tn), lambda l, j: (l, 0, j)),
            pl.BlockSpec((1, 1, tn), lambda l, j: (l, 0, j)),
        ],
        out_specs=pl.BlockSpec((1, b, tn), lambda l, j: (l, 0, j)),
        compiler_params=_params("parallel", "parallel"),
        name="adaln_modulation",
    )(c, ada_w, ada_b.reshape(depth, 1, n))


def _norm_mod(x, g, sc, sh):
    ms = jnp.mean(x * x, axis=-1, keepdims=True)
    return (x * lax.rsqrt(ms + EPS) * g) * (1.0 + sc) + sh


def _nmm_kernel(x_ref, g_ref, sh_ref, sc_ref, w_ref, o_ref, h_scr):
    @pl.when(pl.program_id(1) == 0)
    def _():
        h_scr[...] = _norm_mod(x_ref[...], g_ref[...], sc_ref[0], sh_ref[0]).astype(BF16)

    o_ref[...] = _dot(h_scr[...], w_ref[...]).astype(o_ref.dtype)


def norm_mod_matmul(x2, g, mod3, layer, w, *, rows_per_batch, tm, tn, out_dtype):
    m, d = x2.shape
    n = w.shape[1]
    assert m % tm == 0 and n % tn == 0 and rows_per_batch % tm == 0
    nb = m // rows_per_batch
    blocks_per_batch = rows_per_batch // tm

    def mod_map(chunk):
        return lambda i, j: (layer * nb + i // blocks_per_batch, 0, chunk)

    return pl.pallas_call(
        _nmm_kernel,
        out_shape=jax.ShapeDtypeStruct((m, n), out_dtype),
        grid=(m // tm, n // tn),
        in_specs=[
            pl.BlockSpec((tm, d), lambda i, j: (i, 0)),
            pl.BlockSpec((1, d), lambda i, j: (0, 0)),
            pl.BlockSpec((1, 1, d), mod_map(0)),
            pl.BlockSpec((1, 1, d), mod_map(1)),
            pl.BlockSpec((d, tn), lambda i, j: (0, j)),
        ],
        out_specs=pl.BlockSpec((tm, tn), lambda i, j: (i, j)),
        scratch_shapes=[pltpu.VMEM((tm, d), BF16)],
        compiler_params=_params("parallel", "arbitrary"),
        name="norm_mod_matmul",
    )(x2, g.reshape(1, d), mod3, mod3, w)


def _ffn_kernel(x_ref, g_ref, sh_ref, sc_ref, gt_ref, wg_ref, wu_ref, wo_ref, o_ref, h_scr):
    j = pl.program_id(1)

    @pl.when(j == 0)
    def _():
        h_scr[...] = _norm_mod(x_ref[...], g_ref[...], sc_ref[0], sh_ref[0]).astype(BF16)
        o_ref[...] = jnp.zeros_like(o_ref)

    h = h_scr[...]
    act = (_silu(_dot(h, wg_ref[...])) * _dot(h, wu_ref[...])).astype(BF16)
    o_ref[...] += _dot(act, wo_ref[...])

    @pl.when(j == pl.num_programs(1) - 1)
    def _():
        o_ref[...] = x_ref[...] + gt_ref[0] * o_ref[...]


def ffn_sublayer(x2, g, mod3, layer, w_in, w_out, *, rows_per_batch, tm, tf):
    m, d = x2.shape
    d_ff = w_out.shape[0]
    assert m % tm == 0 and d_ff % tf == 0 and rows_per_batch % tm == 0
    nb = m // rows_per_batch
    blocks_per_batch = rows_per_batch // tm
    nj = d_ff // tf

    def mod_map(chunk):
        return lambda i, j: (layer * nb + i // blocks_per_batch, 0, chunk)

    return pl.pallas_call(
        _ffn_kernel,
        out_shape=jax.ShapeDtypeStruct((m, d), F32),
        grid=(m // tm, nj),
        in_specs=[
            pl.BlockSpec((tm, d), lambda i, j: (i, 0)),
            pl.BlockSpec((1, d), lambda i, j: (0, 0)),
            pl.BlockSpec((1, 1, d), mod_map(3)),
            pl.BlockSpec((1, 1, d), mod_map(4)),
            pl.BlockSpec((1, 1, d), mod_map(5)),
            pl.BlockSpec((d, tf), lambda i, j: (0, j)),
            pl.BlockSpec((d, tf), lambda i, j: (0, j + nj)),
            pl.BlockSpec((tf, d), lambda i, j: (j, 0)),
        ],
        out_specs=pl.BlockSpec((tm, d), lambda i, j: (i, 0)),
        scratch_shapes=[pltpu.VMEM((tm, d), BF16)],
        compiler_params=_params("parallel", "arbitrary"),
        name="ffn_sublayer",
    )(x2, g.reshape(1, d), mod3, mod3, mod3, w_in, w_in, w_out)


def _mgr_kernel(a_ref, w_ref, x_ref, gt_ref, o_ref):
    o_ref[...] = x_ref[...] + gt_ref[0] * _dot(a_ref[...], w_ref[...])


def matmul_gate_residual(a, w, x2, mod3, layer, *, rows_per_batch, tm):
    m, k = a.shape
    d = w.shape[1]
    assert m % tm == 0 and rows_per_batch % tm == 0
    nb = m // rows_per_batch
    blocks_per_batch = rows_per_batch // tm
    return pl.pallas_call(
        _mgr_kernel,
        out_shape=jax.ShapeDtypeStruct((m, d), F32),
        grid=(m // tm,),
        in_specs=[
            pl.BlockSpec((tm, k), lambda i: (i, 0)),
            pl.BlockSpec((k, d), lambda i: (0, 0)),
            pl.BlockSpec((tm, d), lambda i: (i, 0)),
            pl.BlockSpec((1, 1, d), lambda i: (layer * nb + i // blocks_per_batch, 0, 2)),
        ],
        out_specs=pl.BlockSpec((tm, d), lambda i: (i, 0)),
        compiler_params=_params("parallel"),
        name="matmul_gate_residual",
    )(a, w, x2, mod3)


DN_GROUP = 8


def _dn_kernel(qp_ref, kp_ref, vp_ref, z_ref, ab_ref, cwq_ref, cwk_ref, cwv_ref, alog_ref, dtb_ref, og_ref,
               o_ref, qn_scr, kn_scr, vv_scr, gb_scr, bb_scr, lhs_scr, add_scr, oo_scr, s_scr):
    t = qp_ref.shape[1]
    c = DN_CHUNK
    d = DN_HEAD_DIM
    n_chunks = t // c
    h = pl.program_id(1)

    row = lax.broadcasted_iota(jnp.int32, (t, d), 0)

    def conv_silu(x, cw):
        acc = x * cw[DN_CONV - 1:DN_CONV, :]
        for s in range(1, DN_CONV):
            shifted = jnp.where(row >= s, pltpu.roll(x, s, 0), 0.0)
            acc = acc + shifted * cw[DN_CONV - 1 - s:DN_CONV - s, :]
        return _silu(acc)

    def l2n(x):
        return x * lax.rsqrt(jnp.sum(x * x, axis=-1, keepdims=True) + EPS)

    qn_scr[...] = l2n(conv_silu(qp_ref[0], cwq_ref[...])) * (d ** -0.5)
    kn_scr[...] = l2n(conv_silu(kp_ref[0], cwk_ref[...]))
    vv_scr[...] = conv_silu(vp_ref[0], cwv_ref[...])

    ab = ab_ref[0]
    lane = lax.broadcasted_iota(jnp.int32, (t, LANES), 1)
    g_all = -jnp.exp(alog_ref[...]) * _softplus(ab + dtb_ref[...])
    beta_all = 1.0 / (1.0 + jnp.exp(-ab))
    g_col = jnp.sum(jnp.where(lane == h, g_all, 0.0), axis=-1, keepdims=True)
    b_col = jnp.sum(jnp.where(lane == h + DN_HEADS, beta_all, 0.0), axis=-1, keepdims=True)
    gb_scr[...] = jnp.broadcast_to(g_col, (t, d))
    bb_scr[...] = jnp.broadcast_to(b_col, (t, d))

    ci = lax.broadcasted_iota(jnp.int32, (c, c), 0)
    cj = lax.broadcasted_iota(jnp.int32, (c, c), 1)
    incl = ci >= cj
    strict = ci > cj
    upper_incl = cj >= ci
    eye = ci == cj
    eye_f = jnp.where(eye, 1.0, 0.0)
    tri_ones = jnp.concatenate([jnp.where(incl, 1.0, 0.0), jnp.ones((c, c), F32)], axis=1).astype(BF16)
    tri_ones2 = jnp.concatenate([tri_ones, tri_ones], axis=1)
    zeros_cc = jnp.zeros((c, c), F32)
    half = c // 2
    half_strict = strict & ((ci < half) == (cj < half))
    lower_left = (ci >= half) & (cj < half)
    n_factors = half.bit_length() - 2

    def prep_group(gi, carry):
        grp = range(DN_GROUP)
        ns = [gi * DN_GROUP + u for u in grp]
        rows = [pl.ds(pl.multiple_of(n * c, c), c) for n in ns]
        qn = [qn_scr[r, :] for r in rows]
        kn = [kn_scr[r, :] for r in rows]
        vv = [vv_scr[r, :] for r in rows]
        g_b = [gb_scr[r, :] for r in rows]
        beta_b = [bb_scr[r, :] for r in rows]

        def cum(g):
            top = jnp.concatenate([g, g], axis=1)
            bot = jnp.concatenate([zeros_cc, -jnp.where(upper_incl, g, 0.0)], axis=1)
            hi, lo = _split_bf16(jnp.concatenate([top, bot], axis=0))
            return _dot(tri_ones2, jnp.concatenate([hi, lo], axis=0))

        gd = [cum(g) for g in g_b]
        big_g = [x[:, :c] for x in gd]
        decay = [jnp.where(incl, jnp.exp(x[:, c:]), 0.0) for x in gd]
        kn16 = [k.astype(BF16) for k in kn]
        qk_kk = [_dot_nt(jnp.concatenate([q.astype(BF16), k16], axis=0), k16) for q, k16 in zip(qn, kn16)]
        attn = [jnp.where(incl, r[:c] * dc, 0.0).astype(BF16) for r, dc in zip(qk_kk, decay)]
        a_mat = [r[c:] * bb * dc for r, bb, dc in zip(qk_kk, beta_b, decay)]

        a_diag = [jnp.where(half_strict, a, 0.0) for a in a_mat]
        a_off = [jnp.where(lower_left, a, 0.0).astype(BF16) for a in a_mat]
        x_inv = [eye_f - a for a in a_diag]
        a16 = [a.astype(BF16) for a in a_diag]
        p16 = [_dot(a, a).astype(BF16) for a in a16]
        for _ in range(n_factors - 1):
            r = [_dot(jnp.concatenate([p, x.astype(BF16)], axis=0), p) for p, x in zip(p16, x_inv)]
            x_inv = [x + y[c:] for x, y in zip(x_inv, r)]
            p16 = [y[:c].astype(BF16) for y in r]
        x_inv = [x + _dot(x.astype(BF16), p) for x, p in zip(x_inv, p16)]
        x16 = [x.astype(BF16) for x in x_inv]
        off = [_dot(a, x).astype(BF16) for a, x in zip(a_off, x16)]
        x_inv = [x - _dot(x_b, y) for x, x_b, y in zip(x_inv, x16, off)]

        exp_g = [jnp.exp(g) for g in big_g]
        rhs = [jnp.concatenate([v * bb, k * bb * eg], axis=1).astype(BF16)
               for v, k, bb, eg in zip(vv, kn, beta_b, exp_g)]
        uw = [_dot(x.astype(BF16), r).astype(BF16) for x, r in zip(x_inv, rhs)]
        g_last = [g[c - 1:c, :] for g in big_g]
        k_tail = [(k * jnp.exp(gl - g)).astype(BF16) for k, gl, g in zip(kn, g_last, big_g)]
        kuw = [_dot_tn(kt, y) for kt, y in zip(k_tail, uw)]
        auw = [_dot(at, y) for at, y in zip(attn, uw)]
        for u in grp:
            m_mat = jnp.where(eye, jnp.exp(g_last[u]), 0.0) - kuw[u][:, d:]
            q_eff = qn[u] * exp_g[u] - auw[u][:, d:]
            lhs_scr[ns[u]] = jnp.concatenate([m_mat, q_eff], axis=0).astype(BF16)
            add_scr[ns[u]] = jnp.concatenate([kuw[u][:, :d], auw[u][:, :d]], axis=0)
        return carry

    lax.fori_loop(0, n_chunks // DN_GROUP, prep_group, 0)

    s_scr[...] = jnp.zeros((d, d), BF16)

    def step(n, carry):
        r = _dot(lhs_scr[n], s_scr[...]) + add_scr[n]
        oo_scr[pl.ds(pl.multiple_of(n * c, c), c), :] = r[d:, :]
        s_scr[...] = r[:d, :].astype(BF16)
        return carry

    lax.fori_loop(0, n_chunks, step, 0)

    o = oo_scr[...]
    on = o * lax.rsqrt(jnp.mean(o * o, axis=-1, keepdims=True) + EPS) * og_ref[...]
    o_ref[0] = (on * _silu(z_ref[0])).astype(o_ref.dtype)


def deltanet_core(proj, conv_w, a_log, dt_bias, onorm_g):
    b, t, _ = proj.shape
    hh, d, c = DN_HEADS, DN_HEAD_DIM, DN_CHUNK
    assert t % (c * DN_GROUP) == 0 and c == d == LANES and 2 * hh <= LANES
    n_chunks = t // c
    alog_pad = jnp.zeros((1, LANES), F32).at[0, :hh].set(a_log)
    dtb_pad = jnp.zeros((1, LANES), F32).at[0, :hh].set(dt_bias)

    def col(block):
        return pl.BlockSpec((1, t, d), lambda bi, hi: (bi, 0, block * hh + hi))

    def cw(block):
        return pl.BlockSpec((DN_CONV, d), lambda bi, hi: (0, block * hh + hi))

    vec = pl.BlockSpec((1, LANES), lambda bi, hi: (0, 0))
    return pl.pallas_call(
        _dn_kernel,
        out_shape=jax.ShapeDtypeStruct((b, t, hh * d), BF16),
        grid=(b, hh),
        in_specs=[col(0), col(1), col(2), col(3),
                  pl.BlockSpec((1, t, LANES), lambda bi, hi: (bi, 0, 4 * hh)),
                  cw(0), cw(1), cw(2), vec, vec, vec],
        out_specs=pl.BlockSpec((1, t, d), lambda bi, hi: (bi, 0, hi)),
        scratch_shapes=[
            pltpu.VMEM((t, d), F32), pltpu.VMEM((t, d), F32), pltpu.VMEM((t, d), F32),
            pltpu.VMEM((t, d), F32), pltpu.VMEM((t, d), F32),
            pltpu.VMEM((n_chunks, d + c, d), BF16), pltpu.VMEM((n_chunks, d + c, d), F32),
            pltpu.VMEM((t, d), F32), pltpu.VMEM((d, d), BF16),
        ],
        compiler_params=_params("parallel", "parallel"),
        name="deltanet_core",
    )(proj, proj, proj, proj, proj, conv_w, conv_w, conv_w, alog_pad, dtb_pad, onorm_g.reshape(1, d))


def _sb_kernel(q_ref, k_ref, v_ref, qg_ref, kg_ref, o_ref, kn_scr, vm_scr, acc_scr, carry_scr):
    blk = SB_BLOCK
    qt_rows = q_ref.shape[1]
    sub_blocks = qt_rows // blk
    qt = pl.program_id(2)
    first = lax.broadcasted_iota(jnp.int32, (1, LANES), 1) < SB_HEAD_DIM
    gi = lax.broadcasted_iota(jnp.int32, (LANES, LANES), 0) < SB_HEAD_DIM
    gj = lax.broadcasted_iota(jnp.int32, (LANES, LANES), 1) < SB_HEAD_DIM
    group_mean1 = jnp.where(gi == gj, 1.0 / SB_HEAD_DIM, 0.0)
    group_mean = jnp.concatenate([group_mean1, group_mean1], axis=0).astype(BF16)

    def head_rms(x, g):
        sq_hi, sq_lo = _split_bf16(x * x)
        ms = _dot(jnp.concatenate([sq_hi, sq_lo], axis=1), group_mean)
        return x * lax.rsqrt(ms + EPS) * g

    @pl.when(qt == 0)
    def _():
        kn_scr[...] = head_rms(k_ref[0].astype(F32), kg_ref[...]).astype(BF16)
        v = v_ref[0]
        vm_scr[0] = jnp.where(first, v, jnp.zeros_like(v))
        vm_scr[1] = jnp.where(first, jnp.zeros_like(v), v)

    qn = head_rms(q_ref[0].astype(F32), qg_ref[...]) * (SB_HEAD_DIM ** -0.5 * LOG2_E)
    q_heads = [jnp.where(first, qn, 0.0).astype(BF16), jnp.where(first, 0.0, qn).astype(BF16)]

    ri = lax.broadcasted_iota(jnp.int32, (blk, blk), 0)
    rj = lax.broadcasted_iota(jnp.int32, (blk, blk), 1)
    suffix1 = jnp.concatenate([jnp.where(ri > rj, 1.0, 0.0), jnp.ones((blk, blk), F32)], axis=1)
    suffix = jnp.concatenate([suffix1, suffix1], axis=0).astype(BF16)

    acc_scr[...] = jnp.zeros_like(acc_scr)
    carry_scr[...] = jnp.zeros_like(carry_scr)

    causal = rj < ri

    def keep(mask, x, n_masked):
        if n_masked is None:
            return jnp.where(mask, x, 0.0)
        if n_masked == x.shape[0]:
            return jnp.where(mask, x, 0.0)
        return jnp.concatenate([jnp.where(mask, x[:n_masked], 0.0), x[n_masked:]], axis=0)

    def sweep(j, r0, n_rows, diagonal, row_ok=None):
        krows = pl.ds(pl.multiple_of(j * blk, blk), blk)
        rows = slice(r0, r0 + n_rows)
        pv = []
        for p in range(2):
            z = _dot_nt(q_heads[p][rows, :], kn_scr[krows, :])
            neg_abs = pltpu.bitcast(pltpu.bitcast(z, jnp.uint32) | jnp.uint32(0x80000000), F32)
            sp = jnp.maximum(z, 0.0) + jnp.log2(1.0 + jnp.exp2(neg_abs))
            sp_m = sp
            if diagonal:
                sp_m = keep(causal, sp_m, blk)
            if row_ok is not None:
                sp_m = keep(row_ok, sp_m, None)
            hi, lo = _split_bf16(sp_m)
            cs = _dot(jnp.concatenate([hi, lo], axis=1), suffix)
            a = jnp.exp2(z - sp - cs[:, :blk] - carry_scr[p, rows, :])
            if diagonal:
                a = keep(causal, a, blk)
            if row_ok is not None:
                a = keep(row_ok, a, None)
            pv.append(_dot(a.astype(BF16), vm_scr[p, krows, :]))
            carry_scr[p, rows, :] += cs[:, blk:]
        acc_scr[rows, :] += pv[0] + pv[1]

    base = qt * sub_blocks
    for jj in reversed(range(sub_blocks)):
        sweep(base + jj, jj * blk, min(SB_BAND, sub_blocks - jj) * blk, True)

    @pl.when(qt > 0)
    def _():
        for dist in range(1, SB_BAND):
            sweep(base - dist, 0, (SB_BAND - dist) * blk, False)

    row_id = lax.broadcasted_iota(jnp.int32, (qt_rows, 1), 0)
    has_far_keys = row_id + base * blk >= SB_BAND * blk

    def least_carry():
        c = jnp.minimum(carry_scr[0], carry_scr[1])
        return jnp.min(jnp.where(has_far_keys, c, SB_ZERO_LOG2))

    def far_cond(state):
        j, least = state
        return jnp.logical_and(j >= 0, least < SB_ZERO_LOG2)

    def far_body(state):
        j, _ = state
        sweep(j, 0, qt_rows, False, row_id >= (j - base + SB_BAND) * blk)
        return j - 1, least_carry()

    lax.while_loop(far_cond, far_body, (base + sub_blocks - 1 - SB_BAND, least_carry()))
    o_ref[0] = acc_scr[...].astype(o_ref.dtype)


def stickbreak_core(qkv, q_norm_g, k_norm_g):
    b, t, _ = qkv.shape
    pairs = SB_HEADS * SB_HEAD_DIM // LANES
    qt_rows = min(SB_QTILE, t)
    assert 2 * SB_HEAD_DIM == LANES and t % qt_rows == 0 and qt_rows % SB_BLOCK == 0
    qg = jnp.tile(q_norm_g, LANES // SB_HEAD_DIM).reshape(1, LANES)
    kg = jnp.tile(k_norm_g, LANES // SB_HEAD_DIM).reshape(1, LANES)
    vec = pl.BlockSpec((1, LANES), lambda bi, hp, qi: (0, 0))
    return pl.pallas_call(
        _sb_kernel,
        out_shape=jax.ShapeDtypeStruct((b, t, pairs * LANES), BF16),
        grid=(b, pairs, t // qt_rows),
        in_specs=[
            pl.BlockSpec((1, qt_rows, LANES), lambda bi, hp, qi: (bi, qi, hp)),
            pl.BlockSpec((1, t, LANES), lambda bi, hp, qi: (bi, 0, pairs + hp)),
            pl.BlockSpec((1, t, LANES), lambda bi, hp, qi: (bi, 0, 2 * pairs + hp)),
            vec, vec,
        ],
        out_specs=pl.BlockSpec((1, qt_rows, LANES), lambda bi, hp, qi: (bi, qi, hp)),
        scratch_shapes=[
            pltpu.VMEM((t, LANES), BF16), pltpu.VMEM((2, t, LANES), BF16),
            pltpu.VMEM((qt_rows, LANES), F32), pltpu.VMEM((2, qt_rows, LANES), F32),
        ],
        compiler_params=_params("parallel", "parallel", "arbitrary"),
        name="stickbreak_core",
    )(qkv, qkv, qkv, qg, kg)


def _pick(n, candidates):
    for cand in candidates:
        if n % cand == 0:
            return cand
    raise ValueError(f"no tile for {n}")


def kernel(x, c, ada_w, ada_b, norm1_g, norm2_g, dn_w_in, dn_conv_w, dn_a_log, dn_dt_bias, dn_onorm_g, dn_w_out,
           sb_w_qkv, sb_q_norm_g, sb_k_norm_g, sb_w_out, ffn_w_in, ffn_w_out):
    b, t, d = x.shape
    depth = ada_w.shape[0]
    m = b * t
    tm = _pick(t, (1024, 512, 256, 128))
    kw = dict(rows_per_batch=t, tm=tm)

    mod = adaln_modulation(c, ada_w, ada_b, tn=_pick(N_MOD * d, (2048, 1024, 512, 128)))
    mod3 = mod.reshape(depth * b, 1, N_MOD * d)

    dn_cols = dn_w_in.shape[-1]
    dn_pad = 4 * DN_HEADS * DN_HEAD_DIM + LANES - dn_cols
    dn_w_in16 = jnp.pad(dn_w_in, ((0, 0), (0, 0), (0, dn_pad))).astype(BF16)
    dn_w_out16 = dn_w_out.astype(BF16)
    sb_w_qkv16 = sb_w_qkv.astype(BF16)
    sb_w_out16 = sb_w_out.astype(BF16)
    ffn_w_in16 = ffn_w_in.astype(BF16)
    ffn_w_out16 = ffn_w_out.astype(BF16)
    d_ff = ffn_w_out.shape[1]

    x2 = x.reshape(m, d)
    for i in range(depth):
        j = i // 2
        if i % 2 == 0:
            n_in = dn_w_in16.shape[-1]
            proj = norm_mod_matmul(x2, norm1_g[i], mod3, i, dn_w_in16[j], out_dtype=F32,
                                   tn=_pick(n_in, (1408, 1024, 768, 384, 128)), **kw)
            y = deltanet_core(proj.reshape(b, t, n_in), dn_conv_w[j], dn_a_log[j], dn_dt_bias[j], dn_onorm_g[j])
            w_out = dn_w_out16[j]
        else:
            n_in = sb_w_qkv16.shape[-1]
            qkv = norm_mod_matmul(x2, norm1_g[i], mod3, i, sb_w_qkv16[j], out_dtype=BF16,
                                  tn=_pick(n_in, (1024, 768, 512, 384, 128)), **kw)
            y = stickbreak_core(qkv.reshape(b, t, n_in), sb_q_norm_g[j], sb_k_norm_g[j])
            w_out = sb_w_out16[j]
        x2 = matmul_gate_residual(y.reshape(m, d), w_out, x2, mod3, i, **kw)
        x2 = ffn_sublayer(x2, norm2_g[i], mod3, i, ffn_w_in16[i], ffn_w_out16[i], rows_per_batch=t,
                          tm=_pick(t, (512, 256, 128)), tf=_pick(d_ff, (1408, 1024, 512, 256, 128)))
    return x2.reshape(b, t, d)
```

```python
import jax
import jax.numpy as jnp
from jax import lax
from jax.experimental import pallas as pl
from jax.experimental.pallas import tpu as pltpu

F32 = jnp.float32
BF16 = jnp.bfloat16

EPS = 1e-6
N_MOD = 6
DN_HEADS = 8
DN_HEAD_DIM = 128
DN_CONV = 4
DN_CHUNK = 128
SB_HEADS = 16
SB_HEAD_DIM = 64
SB_BLOCK = 128
SB_QTILE = 2048
SB_BAND = 3
SB_ZERO_LOG2 = 151.0
LOG2_E = 1.4426950408889634
LANES = 128
VMEM_LIMIT_BYTES = 56 * 1024 * 1024


def _params(*semantics):
    return pltpu.CompilerParams(dimension_semantics=semantics, vmem_limit_bytes=VMEM_LIMIT_BYTES)


def _silu(x):
    return x / (1.0 + jnp.exp(-x))


def _softplus(x):
    return jnp.maximum(x, 0.0) + jnp.log(1.0 + jnp.exp(-jnp.abs(x)))


def _split_bf16(x):
    hi = x.astype(BF16)
    lo = (x - hi.astype(F32)).astype(BF16)
    return hi, lo


def _dot(a, b):
    return jnp.dot(a, b, preferred_element_type=F32)


def _dot_nt(a, b):
    return lax.dot_general(a, b, (((1,), (1,)), ((), ())), preferred_element_type=F32)


def _dot_tn(a, b):
    return lax.dot_general(a, b, (((0,), (0,)), ((), ())), preferred_element_type=F32)


def _adaln_kernel(c_ref, w_ref, b_ref, o_ref):
    cond = _silu(c_ref[...]).astype(BF16)
    o_ref[0] = _dot(cond, w_ref[0].astype(BF16)) + b_ref[0]


def adaln_modulation(c, ada_w, ada_b, *, tn=2048):
    depth, d, n = ada_w.shape
    b = c.shape[0]
    assert n % tn == 0
    return pl.pallas_call(
        _adaln_kernel,
        out_shape=jax.ShapeDtypeStruct((depth, b, n), F32),
        grid=(depth, n // tn),
        in_specs=[
            pl.BlockSpec((b, d), lambda l, j: (0, 0)),
            pl.BlockSpec((1, d, tn), lambda l, j: (l, 0, j)),
            pl.BlockSpec((1, 1, tn), lambda l, j: (l, 0, j)),
        ],
        out_specs=pl.BlockSpec((1, b, tn), lambda l, j: (l, 0, j)),
        compiler_params=_params("parallel", "parallel"),
        name="adaln_modulation",
    )(c, ada_w, ada_b.reshape(depth, 1, n))


def _norm_mod(x, g, sc, sh):
    ms = jnp.mean(x * x, axis=-1, keepdims=True)
    return (x * lax.rsqrt(ms + EPS) * g) * (1.0 + sc) + sh


def _nmm_kernel(x_ref, g_ref, sh_ref, sc_ref, w_ref, o_ref, h_scr):
    @pl.when(pl.program_id(1) == 0)
    def _():
        h_scr[...] = _norm_mod(x_ref[...], g_ref[...], sc_ref[0], sh_ref[0]).astype(BF16)

    o_ref[...] = _dot(h_scr[...], w_ref[...]).astype(o_ref.dtype)


def _nmm_side_kernel(x_ref, g_ref, sh_ref, sc_ref, w_ref, ws_ref, o_ref, os_ref, h_scr):
    @pl.when(pl.program_id(1) == 0)
    def _():
        h = _norm_mod(x_ref[...], g_ref[...], sc_ref[0], sh_ref[0]).astype(BF16)
        h_scr[...] = h
        os_ref[...] = _dot(h, ws_ref[...]).astype(os_ref.dtype)

    o_ref[...] = _dot(h_scr[...], w_ref[...]).astype(o_ref.dtype)


def norm_mod_matmul(x2, g, mod3, layer, w, w_side=None, *, rows_per_batch, tm, tn, out_dtype):
    m, d = x2.shape
    n = w.shape[1]
    assert m % tm == 0 and n % tn == 0 and rows_per_batch % tm == 0
    nb = m // rows_per_batch
    blocks_per_batch = rows_per_batch // tm

    def mod_map(chunk):
        return lambda i, j: (layer * nb + i // blocks_per_batch, 0, chunk)

    in_specs = [
        pl.BlockSpec((tm, d), lambda i, j: (i, 0)),
        pl.BlockSpec((1, d), lambda i, j: (0, 0)),
        pl.BlockSpec((1, 1, d), mod_map(0)),
        pl.BlockSpec((1, 1, d), mod_map(1)),
        pl.BlockSpec((d, tn), lambda i, j: (0, j)),
    ]
    args = [x2, g.reshape(1, d), mod3, mod3, w]
    out_shape = jax.ShapeDtypeStruct((m, n), out_dtype)
    out_specs = pl.BlockSpec((tm, tn), lambda i, j: (i, j))
    kernel_fn = _nmm_kernel
    if w_side is not None:
        ns = w_side.shape[1]
        in_specs.append(pl.BlockSpec((d, ns), lambda i, j: (0, 0)))
        args.append(w_side)
        out_shape = (out_shape, jax.ShapeDtypeStruct((m, ns), F32))
        out_specs = (out_specs, pl.BlockSpec((tm, ns), lambda i, j: (i, 0)))
        kernel_fn = _nmm_side_kernel
    return pl.pallas_call(
        kernel_fn,
        out_shape=out_shape,
        grid=(m // tm, n // tn),
        in_specs=in_specs,
        out_specs=out_specs,
        scratch_shapes=[pltpu.VMEM((tm, d), BF16)],
        compiler_params=_params("parallel", "arbitrary"),
        name="norm_mod_matmul",
    )(*args)


def _ffn_kernel(x_ref, g_ref, sh_ref, sc_ref, gt_ref, wg_ref, wu_ref, wo_ref, o_ref, h_scr):
    j = pl.program_id(1)

    @pl.when(j == 0)
    def _():
        h_scr[...] = _norm_mod(x_ref[...], g_ref[...], sc_ref[0], sh_ref[0]).astype(BF16)
        o_ref[...] = jnp.zeros_like(o_ref)

    h = h_scr[...]
    act = (_silu(_dot(h, wg_ref[...])) * _dot(h, wu_ref[...])).astype(BF16)
    o_ref[...] += _dot(act, wo_ref[...])

    @pl.when(j == pl.num_programs(1) - 1)
    def _():
        o_ref[...] = x_ref[...] + gt_ref[0] * o_ref[...]


def ffn_sublayer(x2, g, mod3, layer, w_in, w_out, *, rows_per_batch, tm, tf):
    m, d = x2.shape
    d_ff = w_out.shape[0]
    assert m % tm == 0 and d_ff % tf == 0 and rows_per_batch % tm == 0
    nb = m // rows_per_batch
    blocks_per_batch = rows_per_batch // tm
    nj = d_ff // tf

    def mod_map(chunk):
        return lambda i, j: (layer * nb + i // blocks_per_batch, 0, chunk)

    return pl.pallas_call(
        _ffn_kernel,
        out_shape=jax.ShapeDtypeStruct((m, d), F32),
        grid=(m // tm, nj),
        in_specs=[
            pl.BlockSpec((tm, d), lambda i, j: (i, 0)),
            pl.BlockSpec((1, d), lambda i, j: (0, 0)),
            pl.BlockSpec((1, 1, d), mod_map(3)),
            pl.BlockSpec((1, 1, d), mod_map(4)),
            pl.BlockSpec((1, 1, d), mod_map(5)),
            pl.BlockSpec((d, tf), lambda i, j: (0, j)),
            pl.BlockSpec((d, tf), lambda i, j: (0, j + nj)),
            pl.BlockSpec((tf, d), lambda i, j: (j, 0)),
        ],
        out_specs=pl.BlockSpec((tm, d), lambda i, j: (i, 0)),
        scratch_shapes=[pltpu.VMEM((tm, d), BF16)],
        compiler_params=_params("parallel", "arbitrary"),
        name="ffn_sublayer",
    )(x2, g.reshape(1, d), mod3, mod3, mod3, w_in, w_in, w_out)


def _mgr_kernel(a_ref, w_ref, x_ref, gt_ref, o_ref):
    o_ref[...] = x_ref[...] + gt_ref[0] * _dot(a_ref[...], w_ref[...])


def matmul_gate_residual(a, w, x2, mod3, layer, *, rows_per_batch, tm):
    m, k = a.shape
    d = w.shape[1]
    assert m % tm == 0 and rows_per_batch % tm == 0
    nb = m // rows_per_batch
    blocks_per_batch = rows_per_batch // tm
    return pl.pallas_call(
        _mgr_kernel,
        out_shape=jax.ShapeDtypeStruct((m, d), F32),
        grid=(m // tm,),
        in_specs=[
            pl.BlockSpec((tm, k), lambda i: (i, 0)),
            pl.BlockSpec((k, d), lambda i: (0, 0)),
            pl.BlockSpec((tm, d), lambda i: (i, 0)),
            pl.BlockSpec((1, 1, d), lambda i: (layer * nb + i // blocks_per_batch, 0, 2)),
        ],
        out_specs=pl.BlockSpec((tm, d), lambda i: (i, 0)),
        compiler_params=_params("parallel"),
        name="matmul_gate_residual",
    )(a, w, x2, mod3)


DN_GROUP = 8


def _dn_kernel(qp_ref, kp_ref, vp_ref, z_ref, ab_ref, cwq_ref, cwk_ref, cwv_ref, alog_ref, dtb_ref, og_ref,
               o_ref, qn_scr, kn_scr, vv_scr, gb_scr, bb_scr, lhs_scr, add_scr, oo_scr, s_scr):
    t = qp_ref.shape[1]
    c = DN_CHUNK
    d = DN_HEAD_DIM
    n_chunks = t // c
    h = pl.program_id(1)

    row = lax.broadcasted_iota(jnp.int32, (t, d), 0)

    def conv_silu(x, cw):
        acc = x * cw[DN_CONV - 1:DN_CONV, :]
        for s in range(1, DN_CONV):
            shifted = jnp.where(row >= s, pltpu.roll(x, s, 0), 0.0)
            acc = acc + shifted * cw[DN_CONV - 1 - s:DN_CONV - s, :]
        return _silu(acc)

    def l2n(x):
        return x * lax.rsqrt(jnp.sum(x * x, axis=-1, keepdims=True) + EPS)

    qn_scr[...] = l2n(conv_silu(qp_ref[0].astype(F32), cwq_ref[...])) * (d ** -0.5)
    kn_scr[...] = l2n(conv_silu(kp_ref[0].astype(F32), cwk_ref[...]))
    vv_scr[...] = conv_silu(vp_ref[0].astype(F32), cwv_ref[...])

    ab = ab_ref[0]
    lane = lax.broadcasted_iota(jnp.int32, (t, LANES), 1)
    g_all = -jnp.exp(alog_ref[...]) * _softplus(ab + dtb_ref[...])
    beta_all = 1.0 / (1.0 + jnp.exp(-ab))
    g_col = jnp.sum(jnp.where(lane == h, g_all, 0.0), axis=-1, keepdims=True)
    b_col = jnp.sum(jnp.where(lane == h + DN_HEADS, beta_all, 0.0), axis=-1, keepdims=True)
    gb_scr[...] = jnp.broadcast_to(g_col, (t, d))
    bb_scr[...] = jnp.broadcast_to(b_col, (t, d))

    ci = lax.broadcasted_iota(jnp.int32, (c, c), 0)
    cj = lax.broadcasted_iota(jnp.int32, (c, c), 1)
    incl = ci >= cj
    strict = ci > cj
    upper_incl = cj >= ci
    eye = ci == cj
    eye_f = jnp.where(eye, 1.0, 0.0)
    tri_ones = jnp.concatenate([jnp.where(incl, 1.0, 0.0), jnp.ones((c, c), F32)], axis=1).astype(BF16)
    tri_ones2 = jnp.concatenate([tri_ones, tri_ones], axis=1)
    zeros_cc = jnp.zeros((c, c), BF16)
    half = c // 2
    half_strict = strict & ((ci < half) == (cj < half))
    lower_left = (ci >= half) & (cj < half)
    n_factors = half.bit_length() - 2

    def prep_group(gi):
        grp = range(DN_GROUP)
        ns = [gi * DN_GROUP + u for u in grp]
        rows = [slice(n * c, (n + 1) * c) for n in ns]
        qn = [qn_scr[r, :] for r in rows]
        kn = [kn_scr[r, :] for r in rows]
        vv = [vv_scr[r, :] for r in rows]
        g_b = [gb_scr[r, :] for r in rows]
        beta_b = [bb_scr[r, :] for r in rows]

        def cum(g):
            def stacked(part):
                top = jnp.concatenate([part, part], axis=1)
                bot = jnp.concatenate([zeros_cc, jnp.where(upper_incl, -part, zeros_cc)], axis=1)
                return jnp.concatenate([top, bot], axis=0)

            hi, lo = _split_bf16(g)
            return _dot(tri_ones2, jnp.concatenate([stacked(hi), stacked(lo)], axis=0))

        gd = [cum(g) for g in g_b]
        big_g = [x[:, :c] for x in gd]
        decay = [jnp.where(incl, jnp.exp(x[:, c:]), 0.0) for x in gd]
        kn16 = [k.astype(BF16) for k in kn]
        qk_kk = [_dot_nt(jnp.concatenate([q.astype(BF16), k16], axis=0), k16) for q, k16 in zip(qn, kn16)]
        attn = [jnp.where(incl, r[:c] * dc, 0.0).astype(BF16) for r, dc in zip(qk_kk, decay)]
        a_mat = [r[c:] * bb * dc for r, bb, dc in zip(qk_kk, beta_b, decay)]

        a_diag = [jnp.where(half_strict, a, 0.0) for a in a_mat]
        a_off = [jnp.where(lower_left, a, 0.0).astype(BF16) for a in a_mat]
        x_inv = [eye_f - a for a in a_diag]
        a16 = [a.astype(BF16) for a in a_diag]
        p16 = [_dot(a, a).astype(BF16) for a in a16]
        for _ in range(n_factors - 1):
            r = [_dot(jnp.concatenate([p, x.astype(BF16)], axis=0), p) for p, x in zip(p16, x_inv)]
            x_inv = [x + y[c:] for x, y in zip(x_inv, r)]
            p16 = [y[:c].astype(BF16) for y in r]
        x_inv = [x + _dot(x.astype(BF16), p) for x, p in zip(x_inv, p16)]
        x16 = [x.astype(BF16) for x in x_inv]
        off = [_dot(a, x).astype(BF16) for a, x in zip(a_off, x16)]
        x_inv = [x - _dot(x_b, y) for x, x_b, y in zip(x_inv, x16, off)]

        exp_g = [jnp.exp(g) for g in big_g]
        rhs = [jnp.concatenate([v * bb, k * bb * eg], axis=1).astype(BF16)
               for v, k, bb, eg in zip(vv, kn, beta_b, exp_g)]
        uw = [_dot(x.astype(BF16), r).astype(BF16) for x, r in zip(x_inv, rhs)]
        g_last = [g[c - 1:c, :] for g in big_g]
        k_tail = [(k * jnp.exp(gl - g)).astype(BF16) for k, gl, g in zip(kn, g_last, big_g)]
        kuw = [_dot_tn(kt, y) for kt, y in zip(k_tail, uw)]
        auw = [_dot(at, y) for at, y in zip(attn, uw)]
        for u in grp:
            m_mat = jnp.where(eye, jnp.exp(g_last[u]), 0.0) - kuw[u][:, d:]
            q_eff = qn[u] * exp_g[u] - auw[u][:, d:]
            lhs_scr[ns[u]] = jnp.concatenate([m_mat, q_eff], axis=0).astype(BF16)
            add_scr[ns[u]] = jnp.concatenate([kuw[u][:, :d], auw[u][:, :d]], axis=0)

    for gi in range(n_chunks // DN_GROUP):
        prep_group(gi)

    s_scr[...] = jnp.zeros((d, d), BF16)
    for n in range(n_chunks):
        r = _dot(lhs_scr[n], s_scr[...]) + add_scr[n]
        oo_scr[n * c:(n + 1) * c, :] = r[d:, :]
        s_scr[...] = r[:d, :].astype(BF16)

    o = oo_scr[...]
    on = o * lax.rsqrt(jnp.mean(o * o, axis=-1, keepdims=True) + EPS) * og_ref[...]
    o_ref[0] = (on * _silu(z_ref[0].astype(F32))).astype(o_ref.dtype)


def deltanet_core(proj, ab, conv_w, a_log, dt_bias, onorm_g):
    b, t, _ = proj.shape
    hh, d, c = DN_HEADS, DN_HEAD_DIM, DN_CHUNK
    assert t % (c * DN_GROUP) == 0 and c == d == LANES and 2 * hh <= LANES
    n_chunks = t // c
    alog_pad = jnp.zeros((1, LANES), F32).at[0, :hh].set(a_log)
    dtb_pad = jnp.zeros((1, LANES), F32).at[0, :hh].set(dt_bias)

    def col(block):
        return pl.BlockSpec((1, t, d), lambda bi, hi: (bi, 0, block * hh + hi))

    def cw(block):
        return pl.BlockSpec((DN_CONV, d), lambda bi, hi: (0, block * hh + hi))

    vec = pl.BlockSpec((1, LANES), lambda bi, hi: (0, 0))
    return pl.pallas_call(
        _dn_kernel,
        out_shape=jax.ShapeDtypeStruct((b, t, hh * d), BF16),
        grid=(b, hh),
        in_specs=[col(0), col(1), col(2), col(3),
                  pl.BlockSpec((1, t, LANES), lambda bi, hi: (bi, 0, 0)),
                  cw(0), cw(1), cw(2), vec, vec, vec],
        out_specs=pl.BlockSpec((1, t, d), lambda bi, hi: (bi, 0, hi)),
        scratch_shapes=[
            pltpu.VMEM((t, d), F32), pltpu.VMEM((t, d), F32), pltpu.VMEM((t, d), F32),
            pltpu.VMEM((t, d), F32), pltpu.VMEM((t, d), F32),
            pltpu.VMEM((n_chunks, d + c, d), BF16), pltpu.VMEM((n_chunks, d + c, d), F32),
            pltpu.VMEM((t, d), F32), pltpu.VMEM((d, d), BF16),
        ],
        compiler_params=_params("parallel", "parallel"),
        name="deltanet_core",
    )(proj, proj, proj, proj, ab, conv_w, conv_w, conv_w, alog_pad, dtb_pad, onorm_g.reshape(1, d))


def _sb_kernel(q_ref, k_ref, v_ref, qg_ref, kg_ref, o_ref, kn_scr, vm_scr, acc_scr, carry_scr):
    blk = SB_BLOCK
    qt_rows = q_ref.shape[1]
    sub_blocks = qt_rows // blk
    qt = pl.program_id(2)
    first = lax.broadcasted_iota(jnp.int32, (1, LANES), 1) < SB_HEAD_DIM
    gi = lax.broadcasted_iota(jnp.int32, (LANES, LANES), 0) < SB_HEAD_DIM
    gj = lax.broadcasted_iota(jnp.int32, (LANES, LANES), 1) < SB_HEAD_DIM
    group_mean1 = jnp.where(gi == gj, 1.0 / SB_HEAD_DIM, 0.0)
    group_mean = jnp.concatenate([group_mean1, group_mean1], axis=0).astype(BF16)

    def head_rms(x, g):
        sq_hi, sq_lo = _split_bf16(x * x)
        ms = _dot(jnp.concatenate([sq_hi, sq_lo], axis=1), group_mean)
        return x * lax.rsqrt(ms + EPS) * g

    @pl.when(qt == 0)
    def _():
        kn_scr[...] = head_rms(k_ref[0].astype(F32), kg_ref[...]).astype(BF16)
        v = v_ref[0]
        vm_scr[0] = jnp.where(first, v, jnp.zeros_like(v))
        vm_scr[1] = jnp.where(first, jnp.zeros_like(v), v)

    qn = head_rms(q_ref[0].astype(F32), qg_ref[...]) * (SB_HEAD_DIM ** -0.5 * LOG2_E)
    q_heads = [jnp.where(first, qn, 0.0).astype(BF16), jnp.where(first, 0.0, qn).astype(BF16)]

    ri = lax.broadcasted_iota(jnp.int32, (blk, blk), 0)
    rj = lax.broadcasted_iota(jnp.int32, (blk, blk), 1)
    suffix1 = jnp.concatenate([jnp.where(ri > rj, 1.0, 0.0), jnp.ones((blk, blk), F32)], axis=1)
    suffix = jnp.concatenate([suffix1, suffix1], axis=0).astype(BF16)

    acc_scr[...] = jnp.zeros_like(acc_scr)
    carry_scr[...] = jnp.zeros_like(carry_scr)

    causal = rj < ri

    def keep(mask, x, n_masked):
        if n_masked is None:
            return jnp.where(mask, x, 0.0)
        if n_masked == x.shape[0]:
            return jnp.where(mask, x, 0.0)
        return jnp.concatenate([jnp.where(mask, x[:n_masked], 0.0), x[n_masked:]], axis=0)

    def sweep(j, r0, n_rows, diagonal, row_ok=None):
        krows = pl.ds(pl.multiple_of(j * blk, blk), blk)
        rows = slice(r0, r0 + n_rows)
        pv = []
        for p in range(2):
            z = _dot_nt(q_heads[p][rows, :], kn_scr[krows, :])
            neg_abs = pltpu.bitcast(pltpu.bitcast(z, jnp.uint32) | jnp.uint32(0x80000000), F32)
            sp = jnp.maximum(z, 0.0) + jnp.log2(1.0 + jnp.exp2(neg_abs))
            sp_m = sp
            if diagonal:
                sp_m = keep(causal, sp_m, blk)
            if row_ok is not None:
                sp_m = keep(row_ok, sp_m, None)
            hi, lo = _split_bf16(sp_m)
            cs = _dot(jnp.concatenate([hi, lo], axis=1), suffix)
            a = jnp.exp2(z - sp - cs[:, :blk] - carry_scr[p, rows, :])
            if diagonal:
                a = keep(causal, a, blk)
            if row_ok is not None:
                a = keep(row_ok, a, None)
            pv.append(_dot(a.astype(BF16), vm_scr[p, krows, :]))
            carry_scr[p, rows, :] += cs[:, blk:]
        acc_scr[rows, :] += pv[0] + pv[1]

    base = qt * sub_blocks
    for jj in reversed(range(sub_blocks)):
        sweep(base + jj, jj * blk, min(SB_BAND, sub_blocks - jj) * blk, True)

    @pl.when(qt > 0)
    def _():
        for dist in range(1, SB_BAND):
            sweep(base - dist, 0, (SB_BAND - dist) * blk, False)

    row_id = lax.broadcasted_iota(jnp.int32, (qt_rows, 1), 0)
    has_far_keys = row_id + base * blk >= SB_BAND * blk

    def least_carry():
        c = jnp.minimum(carry_scr[0], carry_scr[1])
        return jnp.min(jnp.where(has_far_keys, c, SB_ZERO_LOG2))

    def far_cond(state):
        j, least = state
        return jnp.logical_and(j >= 0, least < SB_ZERO_LOG2)

    def far_body(state):
        j, _ = state
        sweep(j, 0, qt_rows, False, row_id >= (j - base + SB_BAND) * blk)
        return j - 1, least_carry()

    lax.while_loop(far_cond, far_body, (base + sub_blocks - 1 - SB_BAND, least_carry()))
    o_ref[0] = acc_scr[...].astype(o_ref.dtype)


def stickbreak_core(qkv, q_norm_g, k_norm_g):
    b, t, _ = qkv.shape
    pairs = SB_HEADS * SB_HEAD_DIM // LANES
    qt_rows = min(SB_QTILE, t)
    assert 2 * SB_HEAD_DIM == LANES and t % qt_rows == 0 and qt_rows % SB_BLOCK == 0
    qg = jnp.tile(q_norm_g, LANES // SB_HEAD_DIM).reshape(1, LANES)
    kg = jnp.tile(k_norm_g, LANES // SB_HEAD_DIM).reshape(1, LANES)
    vec = pl.BlockSpec((1, LANES), lambda bi, hp, qi: (0, 0))
    return pl.pallas_call(
        _sb_kernel,
        out_shape=jax.ShapeDtypeStruct((b, t, pairs * LANES), BF16),
        grid=(b, pairs, t // qt_rows),
        in_specs=[
            pl.BlockSpec((1, qt_rows, LANES), lambda bi, hp, qi: (bi, qi, hp)),
            pl.BlockSpec((1, t, LANES), lambda bi, hp, qi: (bi, 0, pairs + hp)),
            pl.BlockSpec((1, t, LANES), lambda bi, hp, qi: (bi, 0, 2 * pairs + hp)),
            vec, vec,
        ],
        out_specs=pl.BlockSpec((1, qt_rows, LANES), lambda bi, hp, qi: (bi, qi, hp)),
        scratch_shapes=[
            pltpu.VMEM((t, LANES), BF16), pltpu.VMEM((2, t, LANES), BF16),
            pltpu.VMEM((qt_rows, LANES), F32), pltpu.VMEM((2, qt_rows, LANES), F32),
        ],
        compiler_params=_params("parallel", "parallel", "arbitrary"),
        name="stickbreak_core",
    )(qkv, qkv, qkv, qg, kg)


def _pick(n, candidates):
    for cand in candidates:
        if n % cand == 0:
            return cand
    raise ValueError(f"no tile for {n}")


def kernel(x, c, ada_w, ada_b, norm1_g, norm2_g, dn_w_in, dn_conv_w, dn_a_log, dn_dt_bias, dn_onorm_g, dn_w_out,
           sb_w_qkv, sb_q_norm_g, sb_k_norm_g, sb_w_out, ffn_w_in, ffn_w_out):
    b, t, d = x.shape
    depth = ada_w.shape[0]
    m = b * t
    tm = _pick(t, (1024, 512, 256, 128))
    kw = dict(rows_per_batch=t, tm=tm)

    mod = adaln_modulation(c, ada_w, ada_b, tn=_pick(N_MOD * d, (2048, 1024, 512, 128)))
    mod3 = mod.reshape(depth * b, 1, N_MOD * d)

    dn_main = 4 * DN_HEADS * DN_HEAD_DIM
    dn_w_in16 = dn_w_in[:, :, :dn_main].astype(BF16)
    dn_w_ab16 = jnp.pad(dn_w_in[:, :, dn_main:], ((0, 0), (0, 0), (0, LANES - 2 * DN_HEADS))).astype(BF16)
    dn_w_out16 = dn_w_out.astype(BF16)
    sb_w_qkv16 = sb_w_qkv.astype(BF16)
    sb_w_out16 = sb_w_out.astype(BF16)
    ffn_w_in16 = ffn_w_in.astype(BF16)
    ffn_w_out16 = ffn_w_out.astype(BF16)
    d_ff = ffn_w_out.shape[1]

    x2 = x.reshape(m, d)
    for i in range(depth):
        j = i // 2
        if i % 2 == 0:
            proj, ab = norm_mod_matmul(x2, norm1_g[i], mod3, i, dn_w_in16[j], dn_w_ab16[j], out_dtype=BF16,
                                       tn=_pick(dn_main, (1024, 512, 256, 128)), **kw)
            y = deltanet_core(proj.reshape(b, t, dn_main), ab.reshape(b, t, LANES), dn_conv_w[j], dn_a_log[j],
                              dn_dt_bias[j], dn_onorm_g[j])
            w_out = dn_w_out16[j]
        else:
            n_in = sb_w_qkv16.shape[-1]
            qkv = norm_mod_matmul(x2, norm1_g[i], mod3, i, sb_w_qkv16[j], out_dtype=BF16,
                                  tn=_pick(n_in, (1024, 768, 512, 384, 128)), **kw)
            y = stickbreak_core(qkv.reshape(b, t, n_in), sb_q_norm_g[j], sb_k_norm_g[j])
            w_out = sb_w_out16[j]
        x2 = matmul_gate_residual(y.reshape(m, d), w_out, x2, mod3, i, **kw)
        x2 = ffn_sublayer(x2, norm2_g[i], mod3, i, ffn_w_in16[i], ffn_w_out16[i], rows_per_batch=t,
                          tm=_pick(t, (512, 256, 128)), tf=_pick(d_ff, (1408, 1024, 512, 256, 128)))
    return x2.reshape(b, t, d)
```

```python
import jax
import jax.numpy as jnp
from jax import lax
from jax.experimental import pallas as pl
from jax.experimental.pallas import tpu as pltpu

F32 = jnp.float32
BF16 = jnp.bfloat16

EPS = 1e-6
N_MOD = 6
DN_HEADS = 8
DN_HEAD_DIM = 128
DN_CONV = 4
DN_CHUNK = 128
SB_HEADS = 16
SB_HEAD_DIM = 64
SB_BLOCK = 128
SB_QTILE = 2048
SB_BAND = 3
SB_ZERO_LOG2 = 151.0
LOG2_E = 1.4426950408889634
LANES = 128
VMEM_LIMIT_BYTES = 56 * 1024 * 1024


def _params(*semantics):
    return pltpu.CompilerParams(dimension_semantics=semantics, vmem_limit_bytes=VMEM_LIMIT_BYTES)


def _silu(x):
    return x / (1.0 + jnp.exp(-x))


def _softplus(x):
    return jnp.maximum(x, 0.0) + jnp.log(1.0 + jnp.exp(-jnp.abs(x)))


def _split_bf16(x):
    hi = x.astype(BF16)
    lo = (x - hi.astype(F32)).astype(BF16)
    return hi, lo


def _dot(a, b):
    return jnp.dot(a, b, preferred_element_type=F32)


def _dot_nt(a, b):
    return lax.dot_general(a, b, (((1,), (1,)), ((), ())), preferred_element_type=F32)


def _dot_tn(a, b):
    return lax.dot_general(a, b, (((0,), (0,)), ((), ())), preferred_element_type=F32)


def _adaln_kernel(c_ref, w_ref, b_ref, o_ref):
    cond = _silu(c_ref[...]).astype(BF16)
    o_ref[0] = _dot(cond, w_ref[0].astype(BF16)) + b_ref[0]


def adaln_modulation(c, ada_w, ada_b, *, tn=2048):
    depth, d, n = ada_w.shape
    b = c.shape[0]
    assert n % tn == 0
    return pl.pallas_call(
        _adaln_kernel,
        out_shape=jax.ShapeDtypeStruct((depth, b, n), F32),
        grid=(depth, n // tn),
        in_specs=[
            pl.BlockSpec((b, d), lambda l, j: (0, 0)),
            pl.BlockSpec((1, d, tn), lambda l, j: (l, 0, j)),
            pl.BlockSpec((1, 1, tn), lambda l, j: (l, 0, j)),
        ],
        out_specs=pl.BlockSpec((1, b, tn), lambda l, j: (l, 0, j)),
        compiler_params=_params("parallel", "parallel"),
        name="adaln_modulation",
    )(c, ada_w, ada_b.reshape(depth, 1, n))


def _norm_mod(x, g, sc, sh):
    ms = jnp.mean(x * x, axis=-1, keepdims=True)
    return (x * lax.rsqrt(ms + EPS) * g) * (1.0 + sc) + sh


def _nmm_kernel(x_ref, g_ref, sh_ref, sc_ref, w_ref, o_ref):
    h = _norm_mod(x_ref[...], g_ref[...], sc_ref[0], sh_ref[0]).astype(BF16)
    o_ref[...] = _dot(h, w_ref[...]).astype(o_ref.dtype)


def _nmm_side_kernel(x_ref, g_ref, sh_ref, sc_ref, w_ref, ws_ref, o_ref, os_ref):
    h = _norm_mod(x_ref[...], g_ref[...], sc_ref[0], sh_ref[0]).astype(BF16)
    o_ref[...] = _dot(h, w_ref[...]).astype(o_ref.dtype)
    os_ref[...] = _dot(h, ws_ref[...]).astype(os_ref.dtype)


def norm_mod_matmul(x2, g3, mod3, layer, w, w_layer, n, w_side=None, *, rows_per_batch, tm, out_dtype):
    m, d = x2.shape
    assert m % tm == 0 and rows_per_batch % tm == 0 and n % LANES == 0
    nb = m // rows_per_batch
    blocks_per_batch = rows_per_batch // tm

    def mod_map(chunk):
        return lambda i: (layer * nb + i // blocks_per_batch, 0, chunk)

    in_specs = [
        pl.BlockSpec((tm, d), lambda i: (i, 0)),
        pl.BlockSpec((None, 1, d), lambda i: (layer, 0, 0)),
        pl.BlockSpec((1, 1, d), mod_map(0)),
        pl.BlockSpec((1, 1, d), mod_map(1)),
        pl.BlockSpec((None, d, n), lambda i: (w_layer, 0, 0)),
    ]
    args = [x2, g3, mod3, mod3, w]
    out_shape = jax.ShapeDtypeStruct((m, n), out_dtype)
    out_specs = pl.BlockSpec((tm, n), lambda i: (i, 0))
    kernel_fn = _nmm_kernel
    if w_side is not None:
        ns = w_side.shape[2]
        in_specs.append(pl.BlockSpec((None, d, ns), lambda i: (w_layer, 0, 0)))
        args.append(w_side)
        out_shape = (out_shape, jax.ShapeDtypeStruct((m, ns), F32))
        out_specs = (out_specs, pl.BlockSpec((tm, ns), lambda i: (i, 0)))
        kernel_fn = _nmm_side_kernel
    return pl.pallas_call(
        kernel_fn,
        out_shape=out_shape,
        grid=(m // tm,),
        in_specs=in_specs,
        out_specs=out_specs,
        compiler_params=_params("parallel"),
        name="norm_mod_matmul",
    )(*args)


def _ffn_kernel(x_ref, g_ref, sh_ref, sc_ref, gt_ref, wg_ref, wu_ref, wo_ref, o_ref, h_scr):
    j = pl.program_id(1)

    @pl.when(j == 0)
    def _():
        h_scr[...] = _norm_mod(x_ref[...], g_ref[...], sc_ref[0], sh_ref[0]).astype(BF16)
        o_ref[...] = jnp.zeros_like(o_ref)

    h = h_scr[...]
    act = (_silu(_dot(h, wg_ref[...])) * _dot(h, wu_ref[...])).astype(BF16)
    o_ref[...] += _dot(act, wo_ref[...])

    @pl.when(j == pl.num_programs(1) - 1)
    def _():
        o_ref[...] = x_ref[...] + gt_ref[0] * o_ref[...]


def ffn_sublayer(x2, g3, mod3, layer, w_in, w_out, *, rows_per_batch, tm, tf):
    m, d = x2.shape
    d_ff = w_out.shape[1]
    assert m % tm == 0 and d_ff % tf == 0 and rows_per_batch % tm == 0
    nb = m // rows_per_batch
    blocks_per_batch = rows_per_batch // tm
    nj = d_ff // tf

    def mod_map(chunk):
        return lambda i, j: (layer * nb + i // blocks_per_batch, 0, chunk)

    return pl.pallas_call(
        _ffn_kernel,
        out_shape=jax.ShapeDtypeStruct((m, d), F32),
        grid=(m // tm, nj),
        in_specs=[
            pl.BlockSpec((tm, d), lambda i, j: (i, 0)),
            pl.BlockSpec((None, 1, d), lambda i, j: (layer, 0, 0)),
            pl.BlockSpec((1, 1, d), mod_map(3)),
            pl.BlockSpec((1, 1, d), mod_map(4)),
            pl.BlockSpec((1, 1, d), mod_map(5)),
            pl.BlockSpec((None, d, tf), lambda i, j: (layer, 0, j)),
            pl.BlockSpec((None, d, tf), lambda i, j: (layer, 0, j + nj)),
            pl.BlockSpec((None, tf, d), lambda i, j: (layer, j, 0)),
        ],
        out_specs=pl.BlockSpec((tm, d), lambda i, j: (i, 0)),
        scratch_shapes=[pltpu.VMEM((tm, d), BF16)],
        compiler_params=_params("parallel", "arbitrary"),
        name="ffn_sublayer",
    )(x2, g3, mod3, mod3, mod3, w_in, w_in, w_out)


def _mgr_kernel(a_ref, w_ref, x_ref, gt_ref, o_ref):
    o_ref[...] = x_ref[...] + gt_ref[0] * _dot(a_ref[...], w_ref[...])


def matmul_gate_residual(a, w, w_layer, x2, mod3, layer, *, rows_per_batch, tm):
    m, k = a.shape
    d = w.shape[2]
    assert m % tm == 0 and rows_per_batch % tm == 0
    nb = m // rows_per_batch
    blocks_per_batch = rows_per_batch // tm
    return pl.pallas_call(
        _mgr_kernel,
        out_shape=jax.ShapeDtypeStruct((m, d), F32),
        grid=(m // tm,),
        in_specs=[
            pl.BlockSpec((tm, k), lambda i: (i, 0)),
            pl.BlockSpec((None, k, d), lambda i: (w_layer, 0, 0)),
            pl.BlockSpec((tm, d), lambda i: (i, 0)),
            pl.BlockSpec((1, 1, d), lambda i: (layer * nb + i // blocks_per_batch, 0, 2)),
        ],
        out_specs=pl.BlockSpec((tm, d), lambda i: (i, 0)),
        compiler_params=_params("parallel"),
        name="matmul_gate_residual",
    )(a, w, x2, mod3)


DN_GROUP = 8


def _dn_kernel(qp_ref, kp_ref, vp_ref, z_ref, ab_ref, cwq_ref, cwk_ref, cwv_ref, alog_ref, dtb_ref, og_ref,
               o_ref, qn_scr, kn_scr, vv_scr, gb_scr, bb_scr, lhs_scr, add_scr, oo_scr, s_scr):
    t = qp_ref.shape[1]
    c = DN_CHUNK
    d = DN_HEAD_DIM
    n_chunks = t // c
    h = pl.program_id(1)

    head_row = lax.broadcasted_iota(jnp.int32, (8, d), 0)

    def conv_silu(x, cw):
        acc = x * cw[DN_CONV - 1:DN_CONV, :]
        for s in range(1, DN_CONV):
            rolled = pltpu.roll(x, s, 0)
            shifted = jnp.concatenate([jnp.where(head_row >= s, rolled[:8], 0.0), rolled[8:]], axis=0)
            acc = acc + shifted * cw[DN_CONV - 1 - s:DN_CONV - s, :]
        return _silu(acc)

    def l2n(x):
        return x * lax.rsqrt(jnp.sum(x * x, axis=-1, keepdims=True) + EPS)

    qn_scr[...] = l2n(conv_silu(qp_ref[0].astype(F32), cwq_ref[...])) * (d ** -0.5)
    kn_scr[...] = l2n(conv_silu(kp_ref[0].astype(F32), cwk_ref[...]))
    vv_scr[...] = conv_silu(vp_ref[0].astype(F32), cwv_ref[...])

    ab = ab_ref[0]
    lane = lax.broadcasted_iota(jnp.int32, (t, LANES), 1)
    g_all = -jnp.exp(alog_ref[...]) * _softplus(ab + dtb_ref[...])
    beta_all = 1.0 / (1.0 + jnp.exp(-ab))
    g_col = jnp.sum(jnp.where(lane == h, g_all, 0.0), axis=-1, keepdims=True)
    b_col = jnp.sum(jnp.where(lane == h + DN_HEADS, beta_all, 0.0), axis=-1, keepdims=True)
    gb_scr[...] = jnp.broadcast_to(g_col, (t, d))
    bb_scr[...] = jnp.broadcast_to(b_col, (t, d))

    ci = lax.broadcasted_iota(jnp.int32, (c, c), 0)
    cj = lax.broadcasted_iota(jnp.int32, (c, c), 1)
    incl = ci >= cj
    strict = ci > cj
    upper_incl = cj >= ci
    eye = ci == cj
    eye_f = jnp.where(eye, 1.0, 0.0)
    tri_ones = jnp.concatenate([jnp.where(incl, 1.0, 0.0), jnp.ones((c, c), F32)], axis=1).astype(BF16)
    tri_ones2 = jnp.concatenate([tri_ones, tri_ones], axis=1)
    zeros_cc = jnp.zeros((c, c), BF16)
    half = c // 2
    half_strict = strict & ((ci < half) == (cj < half))
    lower_left = (ci >= half) & (cj < half)
    n_factors = half.bit_length() - 2

    def prep_group(gi):
        grp = range(DN_GROUP)
        ns = [gi * DN_GROUP + u for u in grp]
        rows = [slice(n * c, (n + 1) * c) for n in ns]
        qn = [qn_scr[r, :] for r in rows]
        kn = [kn_scr[r, :] for r in rows]
        vv = [vv_scr[r, :] for r in rows]
        g_b = [gb_scr[r, :] for r in rows]
        beta_b = [bb_scr[r, :] for r in rows]

        def cum(g):
            def stacked(part):
                top = jnp.concatenate([part, part], axis=1)
                bot = jnp.concatenate([zeros_cc, jnp.where(upper_incl, -part, zeros_cc)], axis=1)
                return jnp.concatenate([top, bot], axis=0)

            hi, lo = _split_bf16(g)
            return _dot(tri_ones2, jnp.concatenate([stacked(hi), stacked(lo)], axis=0))

        gd = [cum(g) for g in g_b]
        big_g = [x[:, :c] for x in gd]
        decay = [jnp.where(incl, jnp.exp(x[:, c:]), 0.0) for x in gd]
        kn16 = [k.astype(BF16) for k in kn]
        qk_kk = [_dot_nt(jnp.concatenate([q.astype(BF16), k16], axis=0), k16) for q, k16 in zip(qn, kn16)]
        attn = [jnp.where(incl, r[:c] * dc, 0.0).astype(BF16) for r, dc in zip(qk_kk, decay)]
        a_mat = [r[c:] * bb * dc for r, bb, dc in zip(qk_kk, beta_b, decay)]

        a_diag = [jnp.where(half_strict, a, 0.0) for a in a_mat]
        a_off = [jnp.where(lower_left, a, 0.0).astype(BF16) for a in a_mat]
        x_inv = [eye_f - a for a in a_diag]
        a16 = [a.astype(BF16) for a in a_diag]
        p16 = [_dot(a, a).astype(BF16) for a in a16]
        for _ in range(n_factors - 1):
            r = [_dot(jnp.concatenate([p, x.astype(BF16)], axis=0), p) for p, x in zip(p16, x_inv)]
            x_inv = [x + y[c:] for x, y in zip(x_inv, r)]
            p16 = [y[:c].astype(BF16) for y in r]
        x_inv = [x + _dot(x.astype(BF16), p) for x, p in zip(x_inv, p16)]
        x16 = [x.astype(BF16) for x in x_inv]
        off = [_dot(a, x).astype(BF16) for a, x in zip(a_off, x16)]
        x_inv = [x - _dot(x_b, y) for x, x_b, y in zip(x_inv, x16, off)]

        exp_g = [jnp.exp(g) for g in big_g]
        rhs = [jnp.concatenate([v * bb, k * bb * eg], axis=1).astype(BF16)
               for v, k, bb, eg in zip(vv, kn, beta_b, exp_g)]
        uw = [_dot(x.astype(BF16), r).astype(BF16) for x, r in zip(x_inv, rhs)]
        g_last = [g[c - 1:c, :] for g in big_g]
        k_tail = [(k * jnp.exp(gl - g)).astype(BF16) for k, gl, g in zip(kn, g_last, big_g)]
        kuw = [_dot_tn(kt, y) for kt, y in zip(k_tail, uw)]
        auw = [_dot(at, y) for at, y in zip(attn, uw)]
        for u in grp:
            m_mat = jnp.where(eye, jnp.exp(g_last[u]), 0.0) - kuw[u][:, d:]
            q_eff = qn[u] * exp_g[u] - auw[u][:, d:]
            lhs_scr[ns[u]] = jnp.concatenate([m_mat, q_eff], axis=0).astype(BF16)
            add_scr[ns[u]] = jnp.concatenate([kuw[u][:, :d], auw[u][:, :d]], axis=0)

    for gi in range(n_chunks // DN_GROUP):
        prep_group(gi)

    s_scr[...] = jnp.zeros((d, d), BF16)
    for n in range(n_chunks):
        r = _dot(lhs_scr[n], s_scr[...]) + add_scr[n]
        oo_scr[n * c:(n + 1) * c, :] = r[d:, :]
        s_scr[...] = r[:d, :].astype(BF16)

    o = oo_scr[...]
    on = o * lax.rsqrt(jnp.mean(o * o, axis=-1, keepdims=True) + EPS) * og_ref[...]
    o_ref[0] = (on * _silu(z_ref[0].astype(F32))).astype(o_ref.dtype)


def deltanet_core(proj, ab, conv_w, a_log, dt_bias, onorm_g):
    b, t, _ = proj.shape
    hh, d, c = DN_HEADS, DN_HEAD_DIM, DN_CHUNK
    assert t % (c * DN_GROUP) == 0 and c == d == LANES and 2 * hh <= LANES
    n_chunks = t // c
    alog_pad = jnp.zeros((1, LANES), F32).at[0, :hh].set(a_log)
    dtb_pad = jnp.zeros((1, LANES), F32).at[0, :hh].set(dt_bias)

    def col(block):
        return pl.BlockSpec((1, t, d), lambda bi, hi: (bi, 0, block * hh + hi))

    def cw(block):
        return pl.BlockSpec((DN_CONV, d), lambda bi, hi: (0, block * hh + hi))

    vec = pl.BlockSpec((1, LANES), lambda bi, hi: (0, 0))
    return pl.pallas_call(
        _dn_kernel,
        out_shape=jax.ShapeDtypeStruct((b, t, hh * d), BF16),
        grid=(b, hh),
        in_specs=[col(0), col(1), col(2), col(3),
                  pl.BlockSpec((1, t, LANES), lambda bi, hi: (bi, 0, 0)),
                  cw(0), cw(1), cw(2), vec, vec, vec],
        out_specs=pl.BlockSpec((1, t, d), lambda bi, hi: (bi, 0, hi)),
        scratch_shapes=[
            pltpu.VMEM((t, d), F32), pltpu.VMEM((t, d), F32), pltpu.VMEM((t, d), F32),
            pltpu.VMEM((t, d), F32), pltpu.VMEM((t, d), F32),
            pltpu.VMEM((n_chunks, d + c, d), BF16), pltpu.VMEM((n_chunks, d + c, d), F32),
            pltpu.VMEM((t, d), F32), pltpu.VMEM((d, d), BF16),
        ],
        compiler_params=_params("parallel", "parallel"),
        name="deltanet_core",
    )(proj, proj, proj, proj, ab, conv_w, conv_w, conv_w, alog_pad, dtb_pad, onorm_g.reshape(1, d))


def _sb_kernel(q_ref, k_ref, v_ref, qg_ref, kg_ref, o_ref, kn_scr, vm_scr, acc_scr, carry_scr):
    blk = SB_BLOCK
    qt_rows = q_ref.shape[1]
    sub_blocks = qt_rows // blk
    qt = pl.program_id(2)
    first = lax.broadcasted_iota(jnp.int32, (1, LANES), 1) < SB_HEAD_DIM
    gi = lax.broadcasted_iota(jnp.int32, (LANES, LANES), 0) < SB_HEAD_DIM
    gj = lax.broadcasted_iota(jnp.int32, (LANES, LANES), 1) < SB_HEAD_DIM
    group_mean1 = jnp.where(gi == gj, 1.0 / SB_HEAD_DIM, 0.0)
    group_mean = jnp.concatenate([group_mean1, group_mean1], axis=0).astype(BF16)

    def head_rms(x, g):
        sq_hi, sq_lo = _split_bf16(x * x)
        ms = _dot(jnp.concatenate([sq_hi, sq_lo], axis=1), group_mean)
        return x * lax.rsqrt(ms + EPS) * g

    @pl.when(qt == 0)
    def _():
        kn_scr[...] = head_rms(k_ref[0].astype(F32), kg_ref[...]).astype(BF16)
        v = v_ref[0]
        vm_scr[0] = jnp.where(first, v, jnp.zeros_like(v))
        vm_scr[1] = jnp.where(first, jnp.zeros_like(v), v)

    qn = head_rms(q_ref[0].astype(F32), qg_ref[...]) * (SB_HEAD_DIM ** -0.5 * LOG2_E)
    q_heads = [jnp.where(first, qn, 0.0).astype(BF16), jnp.where(first, 0.0, qn).astype(BF16)]

    ri = lax.broadcasted_iota(jnp.int32, (blk, blk), 0)
    rj = lax.broadcasted_iota(jnp.int32, (blk, blk), 1)
    suffix1 = jnp.concatenate([jnp.where(ri > rj, 1.0, 0.0), jnp.ones((blk, blk), F32)], axis=1)
    suffix = jnp.concatenate([suffix1, suffix1], axis=0).astype(BF16)

    acc_scr[...] = jnp.zeros_like(acc_scr)
    carry_scr[...] = jnp.zeros_like(carry_scr)

    causal = rj < ri

    def keep(mask, x, n_masked):
        if n_masked is None:
            return jnp.where(mask, x, 0.0)
        if n_masked == x.shape[0]:
            return jnp.where(mask, x, 0.0)
        return jnp.concatenate([jnp.where(mask, x[:n_masked], 0.0), x[n_masked:]], axis=0)

    def sweep(j, r0, n_rows, diagonal, row_ok=None):
        krows = pl.ds(pl.multiple_of(j * blk, blk), blk)
        rows = slice(r0, r0 + n_rows)
        pv = []
        for p in range(2):
            z = _dot_nt(q_heads[p][rows, :], kn_scr[krows, :])
            neg_abs = pltpu.bitcast(pltpu.bitcast(z, jnp.uint32) | jnp.uint32(0x80000000), F32)
            sp = jnp.maximum(z, 0.0) + jnp.log2(1.0 + jnp.exp2(neg_abs))
            sp_m = sp
            if diagonal:
                sp_m = keep(causal, sp_m, blk)
            if row_ok is not None:
                sp_m = keep(row_ok, sp_m, None)
            hi, lo = _split_bf16(sp_m)
            cs = _dot(jnp.concatenate([hi, lo], axis=1), suffix)
            a = jnp.exp2(z - sp - cs[:, :blk] - carry_scr[p, rows, :])
            if diagonal:
                a = keep(causal, a, blk)
            if row_ok is not None:
                a = keep(row_ok, a, None)
            pv.append(_dot(a.astype(BF16), vm_scr[p, krows, :]))
            carry_scr[p, rows, :] += cs[:, blk:]
        acc_scr[rows, :] += pv[0] + pv[1]

    base = qt * sub_blocks
    for jj in reversed(range(sub_blocks)):
        sweep(base + jj, jj * blk, min(SB_BAND, sub_blocks - jj) * blk, True)

    @pl.when(qt > 0)
    def _():
        for dist in range(1, SB_BAND):
            sweep(base - dist, 0, (SB_BAND - dist) * blk, False)

    row_id = lax.broadcasted_iota(jnp.int32, (qt_rows, 1), 0)
    has_far_keys = row_id + base * blk >= SB_BAND * blk

    def least_carry():
        c = jnp.minimum(carry_scr[0], carry_scr[1])
        return jnp.min(jnp.where(has_far_keys, c, SB_ZERO_LOG2))

    def far_cond(state):
        j, least = state
        return jnp.logical_and(j >= 0, least < SB_ZERO_LOG2)

    def far_body(state):
        j, _ = state
        sweep(j, 0, qt_rows, False, row_id >= (j - base + SB_BAND) * blk)
        return j - 1, least_carry()

    lax.while_loop(far_cond, far_body, (base + sub_blocks - 1 - SB_BAND, least_carry()))
    o_ref[0] = acc_scr[...].astype(o_ref.dtype)


def stickbreak_core(qkv, q_norm_g, k_norm_g):
    b, t, _ = qkv.shape
    pairs = SB_HEADS * SB_HEAD_DIM // LANES
    qt_rows = min(SB_QTILE, t)
    assert 2 * SB_HEAD_DIM == LANES and t % qt_rows == 0 and qt_rows % SB_BLOCK == 0
    qg = jnp.tile(q_norm_g, LANES // SB_HEAD_DIM).reshape(1, LANES)
    kg = jnp.tile(k_norm_g, LANES // SB_HEAD_DIM).reshape(1, LANES)
    vec = pl.BlockSpec((1, LANES), lambda bi, hp, qi: (0, 0))
    return pl.pallas_call(
        _sb_kernel,
        out_shape=jax.ShapeDtypeStruct((b, t, pairs * LANES), BF16),
        grid=(b, pairs, t // qt_rows),
        in_specs=[
            pl.BlockSpec((1, qt_rows, LANES), lambda bi, hp, qi: (bi, qi, hp)),
            pl.BlockSpec((1, t, LANES), lambda bi, hp, qi: (bi, 0, pairs + hp)),
            pl.BlockSpec((1, t, LANES), lambda bi, hp, qi: (bi, 0, 2 * pairs + hp)),
            vec, vec,
        ],
        out_specs=pl.BlockSpec((1, qt_rows, LANES), lambda bi, hp, qi: (bi, qi, hp)),
        scratch_shapes=[
            pltpu.VMEM((t, LANES), BF16), pltpu.VMEM((2, t, LANES), BF16),
            pltpu.VMEM((qt_rows, LANES), F32), pltpu.VMEM((2, qt_rows, LANES), F32),
        ],
        compiler_params=_params("parallel", "parallel", "arbitrary"),
        name="stickbreak_core",
    )(qkv, qkv, qkv, qg, kg)


def _pick(n, candidates):
    for cand in candidates:
        if n % cand == 0:
            return cand
    raise ValueError(f"no tile for {n}")


def kernel(x, c, ada_w, ada_b, norm1_g, norm2_g, dn_w_in, dn_conv_w, dn_a_log, dn_dt_bias, dn_onorm_g, dn_w_out,
           sb_w_qkv, sb_q_norm_g, sb_k_norm_g, sb_w_out, ffn_w_in, ffn_w_out):
    b, t, d = x.shape
    depth = ada_w.shape[0]
    m = b * t
    tm_in = _pick(t, (512, 256, 128))
    tm_out = _pick(t, (1024, 512, 256, 128))

    mod = adaln_modulation(c, ada_w, ada_b, tn=_pick(N_MOD * d, (2048, 1024, 512, 128)))
    mod3 = mod.reshape(depth * b, 1, N_MOD * d)
    g1 = norm1_g.reshape(depth, 1, d)
    g2 = norm2_g.reshape(depth, 1, d)

    dn_main = 4 * DN_HEADS * DN_HEAD_DIM
    dn_w_in16 = dn_w_in.astype(BF16)
    dn_w_ab16 = jnp.pad(dn_w_in[:, :, dn_main:], ((0, 0), (0, 0), (0, LANES - 2 * DN_HEADS))).astype(BF16)
    dn_w_out16 = dn_w_out.astype(BF16)
    sb_w_qkv16 = sb_w_qkv.astype(BF16)
    sb_w_out16 = sb_w_out.astype(BF16)
    ffn_w_in16 = ffn_w_in.astype(BF16)
    ffn_w_out16 = ffn_w_out.astype(BF16)
    d_ff = ffn_w_out.shape[1]

    x2 = x.reshape(m, d)
    for i in range(depth):
        j = i // 2
        if i % 2 == 0:
            proj, ab = norm_mod_matmul(x2, g1, mod3, i, dn_w_in16, j, dn_main, dn_w_ab16, rows_per_batch=t,
                                       tm=tm_in, out_dtype=BF16)
            y = deltanet_core(proj.reshape(b, t, dn_main), ab.reshape(b, t, LANES), dn_conv_w[j], dn_a_log[j],
                              dn_dt_bias[j], dn_onorm_g[j])
            w_out = dn_w_out16
        else:
            n_in = sb_w_qkv16.shape[-1]
            qkv = norm_mod_matmul(x2, g1, mod3, i, sb_w_qkv16, j, n_in, rows_per_batch=t, tm=tm_in,
                                  out_dtype=BF16)
            y = stickbreak_core(qkv.reshape(b, t, n_in), sb_q_norm_g[j], sb_k_norm_g[j])
            w_out = sb_w_out16
        x2 = matmul_gate_residual(y.reshape(m, d), w_out, j, x2, mod3, i, rows_per_batch=t, tm=tm_out)
        x2 = ffn_sublayer(x2, g2, mod3, i, ffn_w_in16, ffn_w_out16, rows_per_batch=t, tm=tm_in,
                          tf=_pick(d_ff, (1408, 1024, 512, 256, 128)))
    return x2.reshape(b, t, d)
```

```python
import jax
import jax.numpy as jnp
from jax import lax
from jax.experimental import pallas as pl
from jax.experimental.pallas import tpu as pltpu

F32 = jnp.float32
BF16 = jnp.bfloat16

EPS = 1e-6
N_MOD = 6
DN_HEADS = 8
DN_HEAD_DIM = 128
DN_CONV = 4
DN_CHUNK = 128
SB_HEADS = 16
SB_HEAD_DIM = 64
SB_BLOCK = 128
SB_QTILE = 2048
SB_BAND = 3
SB_ZERO_LOG2 = 151.0
LOG2_E = 1.4426950408889634
LANES = 128
VMEM_LIMIT_BYTES = 56 * 1024 * 1024


def _params(*semantics):
    return pltpu.CompilerParams(dimension_semantics=semantics, vmem_limit_bytes=VMEM_LIMIT_BYTES)


def _silu(x):
    return x / (1.0 + jnp.exp(-x))


def _softplus(x):
    return jnp.maximum(x, 0.0) + jnp.log(1.0 + jnp.exp(-jnp.abs(x)))


def _split_bf16(x):
    hi = x.astype(BF16)
    lo = (x - hi.astype(F32)).astype(BF16)
    return hi, lo


def _dot(a, b):
    return jnp.dot(a, b, preferred_element_type=F32)


def _dot_nt(a, b):
    return lax.dot_general(a, b, (((1,), (1,)), ((), ())), preferred_element_type=F32)


def _dot_tn(a, b):
    return lax.dot_general(a, b, (((0,), (0,)), ((), ())), preferred_element_type=F32)


def _adaln_kernel(c_ref, w_ref, b_ref, o_ref):
    cond = _silu(c_ref[...]).astype(BF16)
    o_ref[0] = _dot(cond, w_ref[0].astype(BF16)) + b_ref[0]


def adaln_modulation(c, ada_w, ada_b, *, tn=2048):
    depth, d, n = ada_w.shape
    b = c.shape[0]
    assert n % tn == 0
    return pl.pallas_call(
        _adaln_kernel,
        out_shape=jax.ShapeDtypeStruct((depth, b, n), F32),
        grid=(depth, n // tn),
        in_specs=[
            pl.BlockSpec((b, d), lambda l, j: (0, 0)),
            pl.BlockSpec((1, d, tn), lambda l, j: (l, 0, j)),
            pl.BlockSpec((1, 1, tn), lambda l, j: (l, 0, j)),
        ],
        out_specs=pl.BlockSpec((1, b, tn), lambda l, j: (l, 0, j)),
        compiler_params=_params("parallel", "parallel"),
        name="adaln_modulation",
    )(c, ada_w, ada_b.reshape(depth, 1, n))


def _norm_mod(x, g, sc, sh):
    ms = jnp.mean(x * x, axis=-1, keepdims=True)
    return (x * lax.rsqrt(ms + EPS) * g) * (1.0 + sc) + sh


def _head_group_mean():
    gi = lax.broadcasted_iota(jnp.int32, (LANES, LANES), 0) < SB_HEAD_DIM
    gj = lax.broadcasted_iota(jnp.int32, (LANES, LANES), 1) < SB_HEAD_DIM
    one = jnp.where(gi == gj, 1.0 / SB_HEAD_DIM, 0.0)
    return jnp.concatenate([one, one], axis=0).astype(BF16)


def _nmm_headnorm_kernel(x_ref, g_ref, sh_ref, sc_ref, w_ref, hg_ref, o_ref):
    h = _norm_mod(x_ref[...], g_ref[...], sc_ref[0], sh_ref[0]).astype(BF16)
    acc = _dot(h, w_ref[...])
    n_norm = hg_ref.shape[1]
    group_mean = _head_group_mean()
    for lo_col in range(0, n_norm, LANES):
        cols = slice(lo_col, lo_col + LANES)
        xb = acc[:, cols]
        sq_hi, sq_lo = _split_bf16(xb * xb)
        ms = _dot(jnp.concatenate([sq_hi, sq_lo], axis=1), group_mean)
        o_ref[:, cols] = (xb * lax.rsqrt(ms + EPS) * hg_ref[:, cols]).astype(o_ref.dtype)
    o_ref[:, n_norm:] = acc[:, n_norm:].astype(o_ref.dtype)


def _nmm_side_kernel(x_ref, g_ref, sh_ref, sc_ref, w_ref, ws_ref, o_ref, os_ref):
    h = _norm_mod(x_ref[...], g_ref[...], sc_ref[0], sh_ref[0]).astype(BF16)
    o_ref[...] = _dot(h, w_ref[...]).astype(o_ref.dtype)
    os_ref[...] = _dot(h, ws_ref[...]).astype(os_ref.dtype)


def norm_mod_matmul(x2, g3, mod3, layer, w, w_layer, n, *, w_side=None, head_gain=None, rows_per_batch, tm,
                    out_dtype):
    assert (w_side is None) != (head_gain is None)
    m, d = x2.shape
    assert m % tm == 0 and rows_per_batch % tm == 0 and n % LANES == 0
    nb = m // rows_per_batch
    blocks_per_batch = rows_per_batch // tm

    def mod_map(chunk):
        return lambda i: (layer * nb + i // blocks_per_batch, 0, chunk)

    in_specs = [
        pl.BlockSpec((tm, d), lambda i: (i, 0)),
        pl.BlockSpec((None, 1, d), lambda i: (layer, 0, 0)),
        pl.BlockSpec((1, 1, d), mod_map(0)),
        pl.BlockSpec((1, 1, d), mod_map(1)),
        pl.BlockSpec((None, d, n), lambda i: (w_layer, 0, 0)),
    ]
    args = [x2, g3, mod3, mod3, w]
    out_shape = jax.ShapeDtypeStruct((m, n), out_dtype)
    out_specs = pl.BlockSpec((tm, n), lambda i: (i, 0))
    if w_side is not None:
        ns = w_side.shape[2]
        in_specs.append(pl.BlockSpec((None, d, ns), lambda i: (w_layer, 0, 0)))
        args.append(w_side)
        out_shape = (out_shape, jax.ShapeDtypeStruct((m, ns), F32))
        out_specs = (out_specs, pl.BlockSpec((tm, ns), lambda i: (i, 0)))
        kernel_fn = _nmm_side_kernel
    else:
        assert head_gain.shape[1] % LANES == 0 and head_gain.shape[1] <= n
        in_specs.append(pl.BlockSpec(head_gain.shape, lambda i: (0, 0)))
        args.append(head_gain)
        kernel_fn = _nmm_headnorm_kernel
    return pl.pallas_call(
        kernel_fn,
        out_shape=out_shape,
        grid=(m // tm,),
        in_specs=in_specs,
        out_specs=out_specs,
        compiler_params=_params("parallel"),
        name="norm_mod_matmul",
    )(*args)


def _ffn_kernel(x_ref, g_ref, sh_ref, sc_ref, gt_ref, wg_ref, wu_ref, wo_ref, o_ref):
    x = x_ref[...]
    h = _norm_mod(x, g_ref[...], sc_ref[0], sh_ref[0]).astype(BF16)
    act = (_silu(_dot(h, wg_ref[...])) * _dot(h, wu_ref[...])).astype(BF16)
    o_ref[...] = x + gt_ref[0] * _dot(act, wo_ref[...])


def ffn_sublayer(x2, g3, mod3, layer, w_in, w_out, *, rows_per_batch, tm):
    m, d = x2.shape
    d_ff = w_out.shape[1]
    assert m % tm == 0 and rows_per_batch % tm == 0 and d_ff % LANES == 0
    nb = m // rows_per_batch
    blocks_per_batch = rows_per_batch // tm

    def mod_map(chunk):
        return lambda i: (layer * nb + i // blocks_per_batch, 0, chunk)

    return pl.pallas_call(
        _ffn_kernel,
        out_shape=jax.ShapeDtypeStruct((m, d), F32),
        grid=(m // tm,),
        in_specs=[
            pl.BlockSpec((tm, d), lambda i: (i, 0)),
            pl.BlockSpec((None, 1, d), lambda i: (layer, 0, 0)),
            pl.BlockSpec((1, 1, d), mod_map(3)),
            pl.BlockSpec((1, 1, d), mod_map(4)),
            pl.BlockSpec((1, 1, d), mod_map(5)),
            pl.BlockSpec((None, d, d_ff), lambda i: (layer, 0, 0)),
            pl.BlockSpec((None, d, d_ff), lambda i: (layer, 0, 1)),
            pl.BlockSpec((None, d_ff, d), lambda i: (layer, 0, 0)),
        ],
        out_specs=pl.BlockSpec((tm, d), lambda i: (i, 0)),
        compiler_params=_params("parallel"),
        name="ffn_sublayer",
    )(x2, g3, mod3, mod3, mod3, w_in, w_in, w_out)


def _mgr_kernel(a_ref, w_ref, x_ref, gt_ref, o_ref):
    o_ref[...] = x_ref[...] + gt_ref[0] * _dot(a_ref[...], w_ref[...])


def matmul_gate_residual(a, w, w_layer, x2, mod3, layer, *, rows_per_batch, tm):
    m, k = a.shape
    d = w.shape[2]
    assert m % tm == 0 and rows_per_batch % tm == 0
    nb = m // rows_per_batch
    blocks_per_batch = rows_per_batch // tm
    return pl.pallas_call(
        _mgr_kernel,
        out_shape=jax.ShapeDtypeStruct((m, d), F32),
        grid=(m // tm,),
        in_specs=[
            pl.BlockSpec((tm, k), lambda i: (i, 0)),
            pl.BlockSpec((None, k, d), lambda i: (w_layer, 0, 0)),
            pl.BlockSpec((tm, d), lambda i: (i, 0)),
            pl.BlockSpec((1, 1, d), lambda i: (layer * nb + i // blocks_per_batch, 0, 2)),
        ],
        out_specs=pl.BlockSpec((tm, d), lambda i: (i, 0)),
        compiler_params=_params("parallel"),
        name="matmul_gate_residual",
    )(a, w, x2, mod3)


DN_GROUP = 8


def _dn_kernel(qp_ref, kp_ref, vp_ref, z_ref, ab_ref, cwq_ref, cwk_ref, cwv_ref, alog_ref, dtb_ref, og_ref,
               o_ref, qn_scr, kn_scr, vv_scr, gb_scr, bb_scr, lhs_scr, add_scr, oo_scr, s_scr):
    t = qp_ref.shape[1]
    c = DN_CHUNK
    d = DN_HEAD_DIM
    n_chunks = t // c
    h = pl.program_id(1)

    head_row = lax.broadcasted_iota(jnp.int32, (8, d), 0)

    def conv_silu(x, cw):
        acc = x * cw[DN_CONV - 1:DN_CONV, :]
        for s in range(1, DN_CONV):
            rolled = pltpu.roll(x, s, 0)
            shifted = jnp.concatenate([jnp.where(head_row >= s, rolled[:8], 0.0), rolled[8:]], axis=0)
            acc = acc + shifted * cw[DN_CONV - 1 - s:DN_CONV - s, :]
        return _silu(acc)

    def l2n(x):
        return x * lax.rsqrt(jnp.sum(x * x, axis=-1, keepdims=True) + EPS)

    qn_scr[...] = l2n(conv_silu(qp_ref[0].astype(F32), cwq_ref[...])) * (d ** -0.5)
    kn_scr[...] = l2n(conv_silu(kp_ref[0].astype(F32), cwk_ref[...]))
    vv_scr[...] = conv_silu(vp_ref[0].astype(F32), cwv_ref[...])

    ab = ab_ref[0]
    lane = lax.broadcasted_iota(jnp.int32, (t, LANES), 1)
    g_all = -jnp.exp(alog_ref[...]) * _softplus(ab + dtb_ref[...])
    beta_all = 1.0 / (1.0 + jnp.exp(-ab))
    g_col = jnp.sum(jnp.where(lane == h, g_all, 0.0), axis=-1, keepdims=True)
    b_col = jnp.sum(jnp.where(lane == h + DN_HEADS, beta_all, 0.0), axis=-1, keepdims=True)
    gb_scr[...] = jnp.broadcast_to(g_col, (t, d))
    bb_scr[...] = jnp.broadcast_to(b_col, (t, d))

    ci = lax.broadcasted_iota(jnp.int32, (c, c), 0)
    cj = lax.broadcasted_iota(jnp.int32, (c, c), 1)
    incl = ci >= cj
    strict = ci > cj
    upper_incl = cj >= ci
    eye = ci == cj
    eye_f = jnp.where(eye, 1.0, 0.0)
    tri_ones = jnp.concatenate([jnp.where(incl, 1.0, 0.0), jnp.ones((c, c), F32)], axis=1).astype(BF16)
    tri_ones2 = jnp.concatenate([tri_ones, tri_ones], axis=1)
    zeros_cc = jnp.zeros((c, c), BF16)
    half = c // 2
    half_strict = strict & ((ci < half) == (cj < half))
    lower_left = (ci >= half) & (cj < half)
    n_factors = half.bit_length() - 2

    def prep_group(gi):
        grp = range(DN_GROUP)
        ns = [gi * DN_GROUP + u for u in grp]
        rows = [slice(n * c, (n + 1) * c) for n in ns]
        qn = [qn_scr[r, :] for r in rows]
        kn = [kn_scr[r, :] for r in rows]
        vv = [vv_scr[r, :] for r in rows]
        g_b = [gb_scr[r, :] for r in rows]
        beta_b = [bb_scr[r, :] for r in rows]

        def cum(g):
            def stacked(part):
                top = jnp.concatenate([part, part], axis=1)
                bot = jnp.concatenate([zeros_cc, jnp.where(upper_incl, -part, zeros_cc)], axis=1)
                return jnp.concatenate([top, bot], axis=0)

            hi, lo = _split_bf16(g)
            return _dot(tri_ones2, jnp.concatenate([stacked(hi), stacked(lo)], axis=0))

        gd = [cum(g) for g in g_b]
        big_g = [x[:, :c] for x in gd]
        decay = [jnp.where(incl, jnp.exp(x[:, c:]), 0.0) for x in gd]
        kn16 = [k.astype(BF16) for k in kn]
        qk_kk = [_dot_nt(jnp.concatenate([q.astype(BF16), k16], axis=0), k16) for q, k16 in zip(qn, kn16)]
        attn = [jnp.where(incl, r[:c] * dc, 0.0).astype(BF16) for r, dc in zip(qk_kk, decay)]
        a_mat = [r[c:] * bb * dc for r, bb, dc in zip(qk_kk, beta_b, decay)]

        a_diag = [jnp.where(half_strict, a, 0.0) for a in a_mat]
        a_off = [jnp.where(lower_left, a, 0.0).astype(BF16) for a in a_mat]
        x_inv = [eye_f - a for a in a_diag]
        a16 = [a.astype(BF16) for a in a_diag]
        p16 = [_dot(a, a).astype(BF16) for a in a16]
        for _ in range(n_factors - 1):
            r = [_dot(jnp.concatenate([p, x.astype(BF16)], axis=0), p) for p, x in zip(p16, x_inv)]
            x_inv = [x + y[c:] for x, y in zip(x_inv, r)]
            p16 = [y[:c].astype(BF16) for y in r]
        x_inv = [x + _dot(x.astype(BF16), p) for x, p in zip(x_inv, p16)]
        x16 = [x.astype(BF16) for x in x_inv]
        off = [_dot(a, x).astype(BF16) for a, x in zip(a_off, x16)]
        x_inv = [x - _dot(x_b, y) for x, x_b, y in zip(x_inv, x16, off)]

        exp_g = [jnp.exp(g) for g in big_g]
        rhs = [jnp.concatenate([v * bb, k * bb * eg], axis=1).astype(BF16)
               for v, k, bb, eg in zip(vv, kn, beta_b, exp_g)]
        uw = [_dot(x.astype(BF16), r).astype(BF16) for x, r in zip(x_inv, rhs)]
        g_last = [g[c - 1:c, :] for g in big_g]
        k_tail = [(k * jnp.exp(gl - g)).astype(BF16) for k, gl, g in zip(kn, g_last, big_g)]
        kuw = [_dot_tn(kt, y) for kt, y in zip(k_tail, uw)]
        auw = [_dot(at, y) for at, y in zip(attn, uw)]
        for u in grp:
            m_mat = jnp.where(eye, jnp.exp(g_last[u]), 0.0) - kuw[u][:, d:]
            q_eff = qn[u] * exp_g[u] - auw[u][:, d:]
            lhs_scr[ns[u]] = jnp.concatenate([m_mat, q_eff], axis=0).astype(BF16)
            add_scr[ns[u]] = jnp.concatenate([kuw[u][:, :d], auw[u][:, :d]], axis=0)

    for gi in range(n_chunks // DN_GROUP):
        prep_group(gi)

    s_scr[...] = jnp.zeros((d, d), BF16)
    for n in range(n_chunks):
        r = _dot(lhs_scr[n], s_scr[...]) + add_scr[n]
        oo_scr[n * c:(n + 1) * c, :] = r[d:, :]
        s_scr[...] = r[:d, :].astype(BF16)

    o = oo_scr[...]
    on = o * lax.rsqrt(jnp.mean(o * o, axis=-1, keepdims=True) + EPS) * og_ref[...]
    o_ref[0] = (on * _silu(z_ref[0].astype(F32))).astype(o_ref.dtype)


def deltanet_core(proj, ab, conv_w, a_log, dt_bias, onorm_g):
    b, t, _ = proj.shape
    hh, d, c = DN_HEADS, DN_HEAD_DIM, DN_CHUNK
    assert t % (c * DN_GROUP) == 0 and c == d == LANES and 2 * hh <= LANES
    n_chunks = t // c
    alog_pad = jnp.zeros((1, LANES), F32).at[0, :hh].set(a_log)
    dtb_pad = jnp.zeros((1, LANES), F32).at[0, :hh].set(dt_bias)

    def col(block):
        return pl.BlockSpec((1, t, d), lambda bi, hi: (bi, 0, block * hh + hi))

    def cw(block):
        return pl.BlockSpec((DN_CONV, d), lambda bi, hi: (0, block * hh + hi))

    vec = pl.BlockSpec((1, LANES), lambda bi, hi: (0, 0))
    return pl.pallas_call(
        _dn_kernel,
        out_shape=jax.ShapeDtypeStruct((b, t, hh * d), BF16),
        grid=(b, hh),
        in_specs=[col(0), col(1), col(2), col(3),
                  pl.BlockSpec((1, t, LANES), lambda bi, hi: (bi, 0, 0)),
                  cw(0), cw(1), cw(2), vec, vec, vec],
        out_specs=pl.BlockSpec((1, t, d), lambda bi, hi: (bi, 0, hi)),
        scratch_shapes=[
            pltpu.VMEM((t, d), F32), pltpu.VMEM((t, d), F32), pltpu.VMEM((t, d), F32),
            pltpu.VMEM((t, d), F32), pltpu.VMEM((t, d), F32),
            pltpu.VMEM((n_chunks, d + c, d), BF16), pltpu.VMEM((n_chunks, d + c, d), F32),
            pltpu.VMEM((t, d), F32), pltpu.VMEM((d, d), BF16),
        ],
        compiler_params=_params("parallel", "parallel"),
        name="deltanet_core",
    )(proj, proj, proj, proj, ab, conv_w, conv_w, conv_w, alog_pad, dtb_pad, onorm_g.reshape(1, d))


def _sb_kernel(q_ref, k_ref, v_ref, o_ref, vm_scr, acc_scr, carry_scr):
    blk = SB_BLOCK
    qt_rows = q_ref.shape[1]
    sub_blocks = qt_rows // blk
    qt = pl.program_id(2)
    first = lax.broadcasted_iota(jnp.int32, (1, LANES), 1) < SB_HEAD_DIM

    @pl.when(qt == 0)
    def _():
        v = v_ref[0]
        vm_scr[0] = jnp.where(first, v, jnp.zeros_like(v))
        vm_scr[1] = jnp.where(first, jnp.zeros_like(v), v)

    q = q_ref[0]
    q_heads = [jnp.where(first, q, jnp.zeros_like(q)), jnp.where(first, jnp.zeros_like(q), q)]

    ri = lax.broadcasted_iota(jnp.int32, (blk, blk), 0)
    rj = lax.broadcasted_iota(jnp.int32, (blk, blk), 1)
    suffix1 = jnp.concatenate([jnp.where(ri > rj, 1.0, 0.0), jnp.ones((blk, blk), F32)], axis=1)
    suffix = jnp.concatenate([suffix1, suffix1], axis=0).astype(BF16)

    acc_scr[...] = jnp.zeros_like(acc_scr)
    carry_scr[...] = jnp.zeros_like(carry_scr)

    causal = rj < ri

    def keep(mask, x, n_masked):
        if n_masked is None:
            return jnp.where(mask, x, 0.0)
        if n_masked == x.shape[0]:
            return jnp.where(mask, x, 0.0)
        return jnp.concatenate([jnp.where(mask, x[:n_masked], 0.0), x[n_masked:]], axis=0)

    def sweep(j, r0, n_rows, diagonal, row_ok=None):
        krows = pl.ds(pl.multiple_of(j * blk, blk), blk)
        rows = slice(r0, r0 + n_rows)
        pv = []
        for p in range(2):
            z = _dot_nt(q_heads[p][rows, :], k_ref[0, krows, :])
            neg_abs = pltpu.bitcast(pltpu.bitcast(z, jnp.uint32) | jnp.uint32(0x80000000), F32)
            sp = jnp.maximum(z, 0.0) + jnp.log2(1.0 + jnp.exp2(neg_abs))
            sp_m = sp
            if diagonal:
                sp_m = keep(causal, sp_m, blk)
            if row_ok is not None:
                sp_m = keep(row_ok, sp_m, None)
            hi, lo = _split_bf16(sp_m)
            cs = _dot(jnp.concatenate([hi, lo], axis=1), suffix)
            a = jnp.exp2(z - sp - cs[:, :blk] - carry_scr[p, rows, :])
            if diagonal:
                a = keep(causal, a, blk)
            if row_ok is not None:
                a = keep(row_ok, a, None)
            pv.append(_dot(a.astype(BF16), vm_scr[p, krows, :]))
            carry_scr[p, rows, :] += cs[:, blk:]
        acc_scr[rows, :] += pv[0] + pv[1]

    base = qt * sub_blocks
    for jj in reversed(range(sub_blocks)):
        sweep(base + jj, jj * blk, min(SB_BAND, sub_blocks - jj) * blk, True)

    @pl.when(qt > 0)
    def _():
        for dist in range(1, SB_BAND):
            sweep(base - dist, 0, (SB_BAND - dist) * blk, False)

    row_id = lax.broadcasted_iota(jnp.int32, (qt_rows, 1), 0)
    has_far_keys = row_id + base * blk >= SB_BAND * blk

    def least_carry():
        c = jnp.minimum(carry_scr[0], carry_scr[1])
        return jnp.min(jnp.where(has_far_keys, c, SB_ZERO_LOG2))

    def far_cond(state):
        j, least = state
        return jnp.logical_and(j >= 0, least < SB_ZERO_LOG2)

    def far_body(state):
        j, _ = state
        sweep(j, 0, qt_rows, False, row_id >= (j - base + SB_BAND) * blk)
        return j - 1, least_carry()

    lax.while_loop(far_cond, far_body, (base + sub_blocks - 1 - SB_BAND, least_carry()))
    o_ref[0] = acc_scr[...].astype(o_ref.dtype)


def stickbreak_core(qkv):
    b, t, _ = qkv.shape
    pairs = SB_HEADS * SB_HEAD_DIM // LANES
    qt_rows = min(SB_QTILE, t)
    assert 2 * SB_HEAD_DIM == LANES and t % qt_rows == 0 and qt_rows % SB_BLOCK == 0
    return pl.pallas_call(
        _sb_kernel,
        out_shape=jax.ShapeDtypeStruct((b, t, pairs * LANES), BF16),
        grid=(b, pairs, t // qt_rows),
        in_specs=[
            pl.BlockSpec((1, qt_rows, LANES), lambda bi, hp, qi: (bi, qi, hp)),
            pl.BlockSpec((1, t, LANES), lambda bi, hp, qi: (bi, 0, pairs + hp)),
            pl.BlockSpec((1, t, LANES), lambda bi, hp, qi: (bi, 0, 2 * pairs + hp)),
        ],
        out_specs=pl.BlockSpec((1, qt_rows, LANES), lambda bi, hp, qi: (bi, qi, hp)),
        scratch_shapes=[
            pltpu.VMEM((2, t, LANES), BF16),
            pltpu.VMEM((qt_rows, LANES), F32), pltpu.VMEM((2, qt_rows, LANES), F32),
        ],
        compiler_params=_params("parallel", "parallel", "arbitrary"),
        name="stickbreak_core",
    )(qkv, qkv, qkv)


def _pick(n, candidates):
    for cand in candidates:
        if n % cand == 0:
            return cand
    raise ValueError(f"no tile for {n}")


def kernel(x, c, ada_w, ada_b, norm1_g, norm2_g, dn_w_in, dn_conv_w, dn_a_log, dn_dt_bias, dn_onorm_g, dn_w_out,
           sb_w_qkv, sb_q_norm_g, sb_k_norm_g, sb_w_out, ffn_w_in, ffn_w_out):
    b, t, d = x.shape
    depth = ada_w.shape[0]
    m = b * t
    tm_in = _pick(t, (512, 256, 128))
    tm_out = _pick(t, (1024, 512, 256, 128))

    mod = adaln_modulation(c, ada_w, ada_b, tn=_pick(N_MOD * d, (2048, 1024, 512, 128)))
    mod3 = mod.reshape(depth * b, 1, N_MOD * d)
    g1 = norm1_g.reshape(depth, 1, d)
    g2 = norm2_g.reshape(depth, 1, d)

    dn_main = 4 * DN_HEADS * DN_HEAD_DIM
    dn_w_in16 = dn_w_in.astype(BF16)
    dn_w_ab16 = jnp.pad(dn_w_in[:, :, dn_main:], ((0, 0), (0, 0), (0, LANES - 2 * DN_HEADS))).astype(BF16)
    dn_w_out16 = dn_w_out.astype(BF16)
    sb_w_qkv16 = sb_w_qkv.astype(BF16)
    sb_w_out16 = sb_w_out.astype(BF16)
    ffn_w_in16 = ffn_w_in.astype(BF16)
    ffn_w_out16 = ffn_w_out.astype(BF16)

    x2 = x.reshape(m, d)
    for i in range(depth):
        j = i // 2
        if i % 2 == 0:
            proj, ab = norm_mod_matmul(x2, g1, mod3, i, dn_w_in16, j, dn_main, w_side=dn_w_ab16, rows_per_batch=t,
                                       tm=tm_in, out_dtype=BF16)
            y = deltanet_core(proj.reshape(b, t, dn_main), ab.reshape(b, t, LANES), dn_conv_w[j], dn_a_log[j],
                              dn_dt_bias[j], dn_onorm_g[j])
            w_out = dn_w_out16
        else:
            n_in = sb_w_qkv16.shape[-1]
            head_gain = jnp.concatenate([jnp.tile(sb_q_norm_g[j] * (SB_HEAD_DIM ** -0.5 * LOG2_E), SB_HEADS),
                                         jnp.tile(sb_k_norm_g[j], SB_HEADS)]).reshape(1, -1)
            qkv = norm_mod_matmul(x2, g1, mod3, i, sb_w_qkv16, j, n_in, head_gain=head_gain, rows_per_batch=t,
                                  tm=tm_in, out_dtype=BF16)
            y = stickbreak_core(qkv.reshape(b, t, n_in))
            w_out = sb_w_out16
        x2 = matmul_gate_residual(y.reshape(m, d), w_out, j, x2, mod3, i, rows_per_batch=t, tm=tm_out)
        x2 = ffn_sublayer(x2, g2, mod3, i, ffn_w_in16, ffn_w_out16, rows_per_batch=t, tm=_pick(t, (256, 128)))
    return x2.reshape(b, t, d)
```

```python
import jax
import jax.numpy as jnp
from jax import lax
from jax.experimental import pallas as pl
from jax.experimental.pallas import tpu as pltpu

F32 = jnp.float32
BF16 = jnp.bfloat16

EPS = 1e-6
N_MOD = 6
DN_HEADS = 8
DN_HEAD_DIM = 128
DN_CONV = 4
DN_CHUNK = 128
SB_HEADS = 16
SB_HEAD_DIM = 64
SB_BLOCK = 128
SB_QTILE = 2048
SB_BAND = 3
SB_ZERO_LOG2 = 151.0
LOG2_E = 1.4426950408889634
LANES = 128
VMEM_LIMIT_BYTES = 56 * 1024 * 1024


def _params(*semantics):
    return pltpu.CompilerParams(dimension_semantics=semantics, vmem_limit_bytes=VMEM_LIMIT_BYTES)


def _silu(x):
    return x / (1.0 + jnp.exp(-x))


def _softplus(x):
    return jnp.maximum(x, 0.0) + jnp.log(1.0 + jnp.exp(-jnp.abs(x)))


def _split_bf16(x):
    hi = x.astype(BF16)
    lo = (x - hi.astype(F32)).astype(BF16)
    return hi, lo


def _dot(a, b):
    return jnp.dot(a, b, preferred_element_type=F32)


def _dot_nt(a, b):
    return lax.dot_general(a, b, (((1,), (1,)), ((), ())), preferred_element_type=F32)


def _dot_tn(a, b):
    return lax.dot_general(a, b, (((0,), (0,)), ((), ())), preferred_element_type=F32)


def _adaln_kernel(c_ref, w_ref, b_ref, o_ref):
    cond = _silu(c_ref[...]).astype(BF16)
    o_ref[0] = _dot(cond, w_ref[0].astype(BF16)) + b_ref[0]


def adaln_modulation(c, ada_w, ada_b, *, tn=2048):
    depth, d, n = ada_w.shape
    b = c.shape[0]
    assert n % tn == 0
    return pl.pallas_call(
        _adaln_kernel,
        out_shape=jax.ShapeDtypeStruct((depth, b, n), F32),
        grid=(depth, n // tn),
        in_specs=[
            pl.BlockSpec((b, d), lambda l, j: (0, 0)),
            pl.BlockSpec((1, d, tn), lambda l, j: (l, 0, j)),
            pl.BlockSpec((1, 1, tn), lambda l, j: (l, 0, j)),
        ],
        out_specs=pl.BlockSpec((1, b, tn), lambda l, j: (l, 0, j)),
        compiler_params=_params("parallel", "parallel"),
        name="adaln_modulation",
    )(c, ada_w, ada_b.reshape(depth, 1, n))


def _norm_mod(x, g, sc, sh):
    ms = jnp.mean(x * x, axis=-1, keepdims=True)
    return (x * lax.rsqrt(ms + EPS) * g) * (1.0 + sc) + sh


def _nmm_headnorm_kernel(x_ref, g_ref, sh_ref, sc_ref, w_ref, hg_ref, o_ref):
    h = _norm_mod(x_ref[...], g_ref[...], sc_ref[0], sh_ref[0]).astype(BF16)
    acc = _dot(h, w_ref[...])
    n_norm = hg_ref.shape[1]
    first = lax.broadcasted_iota(jnp.int32, (1, LANES), 1) < SB_HEAD_DIM
    for lo_col in range(0, n_norm, LANES):
        cols = slice(lo_col, lo_col + LANES)
        xb = acc[:, cols]
        sq = xb * xb
        s_all = jnp.sum(sq, axis=-1, keepdims=True)
        s_first = jnp.sum(jnp.where(first, sq, 0.0), axis=-1, keepdims=True)
        ms = jnp.where(first, s_first, s_all - s_first) * (1.0 / SB_HEAD_DIM)
        o_ref[:, cols] = (xb * lax.rsqrt(ms + EPS) * hg_ref[:, cols]).astype(o_ref.dtype)
    o_ref[:, n_norm:] = acc[:, n_norm:].astype(o_ref.dtype)


def _nmm_side_kernel(x_ref, g_ref, sh_ref, sc_ref, w_ref, ws_ref, o_ref, os_ref):
    h = _norm_mod(x_ref[...], g_ref[...], sc_ref[0], sh_ref[0]).astype(BF16)
    o_ref[...] = _dot(h, w_ref[...]).astype(o_ref.dtype)
    os_ref[...] = _dot(h, ws_ref[...]).astype(os_ref.dtype)


def norm_mod_matmul(x2, g3, mod3, layer, w, w_layer, n, *, w_side=None, head_gain=None, rows_per_batch, tm,
                    out_dtype):
    assert (w_side is None) != (head_gain is None)
    m, d = x2.shape
    assert m % tm == 0 and rows_per_batch % tm == 0 and n % LANES == 0
    nb = m // rows_per_batch
    blocks_per_batch = rows_per_batch // tm

    def mod_map(chunk):
        return lambda i: (layer * nb + i // blocks_per_batch, 0, chunk)

    in_specs = [
        pl.BlockSpec((tm, d), lambda i: (i, 0)),
        pl.BlockSpec((None, 1, d), lambda i: (layer, 0, 0)),
        pl.BlockSpec((1, 1, d), mod_map(0)),
        pl.BlockSpec((1, 1, d), mod_map(1)),
        pl.BlockSpec((None, d, n), lambda i: (w_layer, 0, 0)),
    ]
    args = [x2, g3, mod3, mod3, w]
    out_shape = jax.ShapeDtypeStruct((m, n), out_dtype)
    out_specs = pl.BlockSpec((tm, n), lambda i: (i, 0))
    if w_side is not None:
        ns = w_side.shape[2]
        in_specs.append(pl.BlockSpec((None, d, ns), lambda i: (w_layer, 0, 0)))
        args.append(w_side)
        out_shape = (out_shape, jax.ShapeDtypeStruct((m, ns), F32))
        out_specs = (out_specs, pl.BlockSpec((tm, ns), lambda i: (i, 0)))
        kernel_fn = _nmm_side_kernel
    else:
        assert head_gain.shape[1] % LANES == 0 and head_gain.shape[1] <= n
        in_specs.append(pl.BlockSpec(head_gain.shape, lambda i: (0, 0)))
        args.append(head_gain)
        kernel_fn = _nmm_headnorm_kernel
    return pl.pallas_call(
        kernel_fn,
        out_shape=out_shape,
        grid=(m // tm,),
        in_specs=in_specs,
        out_specs=out_specs,
        compiler_params=_params("parallel"),
        name="norm_mod_matmul",
    )(*args)


def _ffn_kernel(x_ref, g_ref, sh_ref, sc_ref, gt_ref, wg_ref, wu_ref, wo_ref, o_ref):
    x = x_ref[...]
    h = _norm_mod(x, g_ref[...], sc_ref[0], sh_ref[0]).astype(BF16)
    act = (_silu(_dot(h, wg_ref[...])) * _dot(h, wu_ref[...])).astype(BF16)
    o_ref[...] = x + gt_ref[0] * _dot(act, wo_ref[...])


def ffn_sublayer(x2, g3, mod3, layer, w_in, w_out, *, rows_per_batch, tm):
    m, d = x2.shape
    d_ff = w_out.shape[1]
    assert m % tm == 0 and rows_per_batch % tm == 0 and d_ff % LANES == 0
    nb = m // rows_per_batch
    blocks_per_batch = rows_per_batch // tm

    def mod_map(chunk):
        return lambda i: (layer * nb + i // blocks_per_batch, 0, chunk)

    return pl.pallas_call(
        _ffn_kernel,
        out_shape=jax.ShapeDtypeStruct((m, d), F32),
        grid=(m // tm,),
        in_specs=[
            pl.BlockSpec((tm, d), lambda i: (i, 0)),
            pl.BlockSpec((None, 1, d), lambda i: (layer, 0, 0)),
            pl.BlockSpec((1, 1, d), mod_map(3)),
            pl.BlockSpec((1, 1, d), mod_map(4)),
            pl.BlockSpec((1, 1, d), mod_map(5)),
            pl.BlockSpec((None, d, d_ff), lambda i: (layer, 0, 0)),
            pl.BlockSpec((None, d, d_ff), lambda i: (layer, 0, 1)),
            pl.BlockSpec((None, d_ff, d), lambda i: (layer, 0, 0)),
        ],
        out_specs=pl.BlockSpec((tm, d), lambda i: (i, 0)),
        compiler_params=_params("parallel"),
        name="ffn_sublayer",
    )(x2, g3, mod3, mod3, mod3, w_in, w_in, w_out)


def _mgr_kernel(a_ref, w_ref, x_ref, gt_ref, o_ref):
    o_ref[...] = x_ref[...] + gt_ref[0] * _dot(a_ref[...], w_ref[...])


def matmul_gate_residual(a, w, w_layer, x2, mod3, layer, *, rows_per_batch, tm):
    m, k = a.shape
    d = w.shape[2]
    assert m % tm == 0 and rows_per_batch % tm == 0
    nb = m // rows_per_batch
    blocks_per_batch = rows_per_batch // tm
    return pl.pallas_call(
        _mgr_kernel,
        out_shape=jax.ShapeDtypeStruct((m, d), F32),
        grid=(m // tm,),
        in_specs=[
            pl.BlockSpec((tm, k), lambda i: (i, 0)),
            pl.BlockSpec((None, k, d), lambda i: (w_layer, 0, 0)),
            pl.BlockSpec((tm, d), lambda i: (i, 0)),
            pl.BlockSpec((1, 1, d), lambda i: (layer * nb + i // blocks_per_batch, 0, 2)),
        ],
        out_specs=pl.BlockSpec((tm, d), lambda i: (i, 0)),
        compiler_params=_params("parallel"),
        name="matmul_gate_residual",
    )(a, w, x2, mod3)


DN_GROUP = 16


def _dn_kernel(qp_ref, kp_ref, vp_ref, z_ref, ab_ref, cwq_ref, cwk_ref, cwv_ref, alog_ref, dtb_ref, og_ref,
               o_ref, qn_scr, kn_scr, vv_scr, gb_scr, bb_scr, lhs_scr, add_scr, oo_scr, s_scr):
    t = qp_ref.shape[1]
    c = DN_CHUNK
    d = DN_HEAD_DIM
    n_chunks = t // c
    h = pl.program_id(1)

    hist = 16

    def conv_silu(ref, cw, r0):
        if r0 == 0:
            x_ext = jnp.concatenate([jnp.zeros((hist, d), F32), ref[0, 0:c, :].astype(F32)], axis=0)
        else:
            x_ext = ref[0, r0 - hist:r0 + c, :].astype(F32)
        acc = x_ext[hist:] * cw[DN_CONV - 1:DN_CONV, :]
        for s in range(1, DN_CONV):
            acc = acc + pltpu.roll(x_ext, s, 0)[hist:] * cw[DN_CONV - 1 - s:DN_CONV - s, :]
        return _silu(acc)

    def l2n(x):
        return x * lax.rsqrt(jnp.sum(x * x, axis=-1, keepdims=True) + EPS)

    lane = lax.broadcasted_iota(jnp.int32, (c, LANES), 1)
    for n in range(n_chunks):
        r0 = n * c
        rows = slice(r0, r0 + c)
        qn_scr[rows, :] = l2n(conv_silu(qp_ref, cwq_ref[...], r0)) * (d ** -0.5)
        kn_scr[rows, :] = l2n(conv_silu(kp_ref, cwk_ref[...], r0))
        vv_scr[rows, :] = conv_silu(vp_ref, cwv_ref[...], r0)
        ab = ab_ref[0, rows, :]
        g_all = -jnp.exp(alog_ref[...]) * _softplus(ab + dtb_ref[...])
        beta_all = 1.0 / (1.0 + jnp.exp(-ab))
        g_col = jnp.sum(jnp.where(lane == h, g_all, 0.0), axis=-1, keepdims=True)
        b_col = jnp.sum(jnp.where(lane == h + DN_HEADS, beta_all, 0.0), axis=-1, keepdims=True)
        gb_scr[rows, :] = jnp.broadcast_to(g_col, (c, d))
        bb_scr[rows, :] = jnp.broadcast_to(b_col, (c, d))

    ci = lax.broadcasted_iota(jnp.int32, (c, c), 0)
    cj = lax.broadcasted_iota(jnp.int32, (c, c), 1)
    incl = ci >= cj
    strict = ci > cj
    upper_incl = cj >= ci
    eye = ci == cj
    eye_f = jnp.where(eye, 1.0, 0.0)
    tri_ones = jnp.concatenate([jnp.where(incl, 1.0, 0.0), jnp.ones((c, c), F32)], axis=1).astype(BF16)
    tri_ones2 = jnp.concatenate([tri_ones, tri_ones], axis=1)
    zeros_cc = jnp.zeros((c, c), BF16)
    half = c // 2
    half_strict = strict & ((ci < half) == (cj < half))
    lower_left = (ci >= half) & (cj < half)
    n_factors = half.bit_length() - 2
    group = min(DN_GROUP, n_chunks)

    def prep_group(gi):
        grp = range(group)
        ns = [gi * group + u for u in grp]
        rows = [slice(n * c, (n + 1) * c) for n in ns]
        qn = [qn_scr[r, :] for r in rows]
        kn = [kn_scr[r, :] for r in rows]
        vv = [vv_scr[r, :] for r in rows]
        g_b = [gb_scr[r, :] for r in rows]
        beta_b = [bb_scr[r, :] for r in rows]

        def cum(g):
            def stacked(part):
                top = jnp.concatenate([part, part], axis=1)
                bot = jnp.concatenate([zeros_cc, jnp.where(upper_incl, -part, zeros_cc)], axis=1)
                return jnp.concatenate([top, bot], axis=0)

            hi, lo = _split_bf16(g)
            return _dot(tri_ones2, jnp.concatenate([stacked(hi), stacked(lo)], axis=0))

        gd = [cum(g) for g in g_b]
        big_g = [x[:, :c] for x in gd]
        decay = [jnp.where(incl, jnp.exp(x[:, c:]), 0.0) for x in gd]
        kn16 = [k.astype(BF16) for k in kn]
        qk_kk = [_dot_nt(jnp.concatenate([q.astype(BF16), k16], axis=0), k16) for q, k16 in zip(qn, kn16)]
        attn = [jnp.where(incl, r[:c] * dc, 0.0).astype(BF16) for r, dc in zip(qk_kk, decay)]
        a_mat = [r[c:] * bb * dc for r, bb, dc in zip(qk_kk, beta_b, decay)]

        a_diag = [jnp.where(half_strict, a, 0.0) for a in a_mat]
        a_off = [jnp.where(lower_left, a, 0.0).astype(BF16) for a in a_mat]
        x_inv = [eye_f - a for a in a_diag]
        a16 = [a.astype(BF16) for a in a_diag]
        p16 = [_dot(a, a).astype(BF16) for a in a16]
        for _ in range(n_factors - 1):
            r = [_dot(jnp.concatenate([p, x.astype(BF16)], axis=0), p) for p, x in zip(p16, x_inv)]
            x_inv = [x + y[c:] for x, y in zip(x_inv, r)]
            p16 = [y[:c].astype(BF16) for y in r]
        x_inv = [x + _dot(x.astype(BF16), p) for x, p in zip(x_inv, p16)]
        x16 = [x.astype(BF16) for x in x_inv]
        off = [_dot(a, x).astype(BF16) for a, x in zip(a_off, x16)]
        x_inv = [x - _dot(x_b, y) for x, x_b, y in zip(x_inv, x16, off)]

        exp_g = [jnp.exp(g) for g in big_g]
        rhs = [jnp.concatenate([v * bb, k * bb * eg], axis=1).astype(BF16)
               for v, k, bb, eg in zip(vv, kn, beta_b, exp_g)]
        uw = [_dot(x.astype(BF16), r).astype(BF16) for x, r in zip(x_inv, rhs)]
        g_last = [g[c - 1:c, :] for g in big_g]
        k_tail = [(k * jnp.exp(gl - g)).astype(BF16) for k, gl, g in zip(kn, g_last, big_g)]
        kuw = [_dot_tn(kt, y) for kt, y in zip(k_tail, uw)]
        auw = [_dot(at, y) for at, y in zip(attn, uw)]
        for u in grp:
            m_mat = jnp.where(eye, jnp.exp(g_last[u]), 0.0) - kuw[u][:, d:]
            q_eff = qn[u] * exp_g[u] - auw[u][:, d:]
            lhs_scr[ns[u]] = jnp.concatenate([m_mat, q_eff], axis=0).astype(BF16)
            add_scr[ns[u]] = jnp.concatenate([kuw[u][:, :d], auw[u][:, :d]], axis=0)

    for gi in range(n_chunks // group):
        prep_group(gi)

    s_scr[...] = jnp.zeros((d, d), BF16)
    for n in range(n_chunks):
        r = _dot(lhs_scr[n], s_scr[...]) + add_scr[n]
        oo_scr[n * c:(n + 1) * c, :] = r[d:, :]
        s_scr[...] = r[:d, :].astype(BF16)

    o = oo_scr[...]
    on = o * lax.rsqrt(jnp.mean(o * o, axis=-1, keepdims=True) + EPS) * og_ref[...]
    o_ref[0] = (on * _silu(z_ref[0].astype(F32))).astype(o_ref.dtype)


def deltanet_core(proj, ab, conv_w, a_log, dt_bias, onorm_g):
    b, t, _ = proj.shape
    hh, d, c = DN_HEADS, DN_HEAD_DIM, DN_CHUNK
    assert t % c == 0 and (t // c) % min(DN_GROUP, t // c) == 0 and c == d == LANES and 2 * hh <= LANES
    n_chunks = t // c
    alog_pad = jnp.zeros((1, LANES), F32).at[0, :hh].set(a_log)
    dtb_pad = jnp.zeros((1, LANES), F32).at[0, :hh].set(dt_bias)

    def col(block):
        return pl.BlockSpec((1, t, d), lambda bi, hi: (bi, 0, block * hh + hi))

    def cw(block):
        return pl.BlockSpec((DN_CONV, d), lambda bi, hi: (0, block * hh + hi))

    vec = pl.BlockSpec((1, LANES), lambda bi, hi: (0, 0))
    return pl.pallas_call(
        _dn_kernel,
        out_shape=jax.ShapeDtypeStruct((b, t, hh * d), BF16),
        grid=(b, hh),
        in_specs=[col(0), col(1), col(2), col(3),
                  pl.BlockSpec((1, t, LANES), lambda bi, hi: (bi, 0, 0)),
                  cw(0), cw(1), cw(2), vec, vec, vec],
        out_specs=pl.BlockSpec((1, t, d), lambda bi, hi: (bi, 0, hi)),
        scratch_shapes=[
            pltpu.VMEM((t, d), F32), pltpu.VMEM((t, d), F32), pltpu.VMEM((t, d), F32),
            pltpu.VMEM((t, d), F32), pltpu.VMEM((t, d), F32),
            pltpu.VMEM((n_chunks, d + c, d), BF16), pltpu.VMEM((n_chunks, d + c, d), F32),
            pltpu.VMEM((t, d), F32), pltpu.VMEM((d, d), BF16),
        ],
        compiler_params=_params("parallel", "parallel"),
        name="deltanet_core",
    )(proj, proj, proj, proj, ab, conv_w, conv_w, conv_w, alog_pad, dtb_pad, onorm_g.reshape(1, d))


def _sb_kernel(q_ref, k_ref, v_ref, o_ref, vm_scr, acc_scr, carry_scr):
    blk = SB_BLOCK
    qt_rows = q_ref.shape[1]
    sub_blocks = qt_rows // blk
    qt = pl.program_id(2)
    first = lax.broadcasted_iota(jnp.int32, (1, LANES), 1) < SB_HEAD_DIM

    @pl.when(qt == 0)
    def _():
        v = v_ref[0]
        vm_scr[0] = jnp.where(first, v, jnp.zeros_like(v))
        vm_scr[1] = jnp.where(first, jnp.zeros_like(v), v)

    q = q_ref[0]
    q_heads = [jnp.where(first, q, jnp.zeros_like(q)), jnp.where(first, jnp.zeros_like(q), q)]

    ri = lax.broadcasted_iota(jnp.int32, (blk, blk), 0)
    rj = lax.broadcasted_iota(jnp.int32, (blk, blk), 1)
    suffix1 = jnp.concatenate([jnp.where(ri > rj, 1.0, 0.0), jnp.ones((blk, blk), F32)], axis=1)
    suffix = jnp.concatenate([suffix1, suffix1], axis=0).astype(BF16)

    acc_scr[...] = jnp.zeros_like(acc_scr)
    carry_scr[...] = jnp.zeros_like(carry_scr)

    causal = rj < ri

    def keep(mask, x, n_masked):
        if n_masked is None:
            return jnp.where(mask, x, 0.0)
        if n_masked == x.shape[0]:
            return jnp.where(mask, x, 0.0)
        return jnp.concatenate([jnp.where(mask, x[:n_masked], 0.0), x[n_masked:]], axis=0)

    def sweep(j, r0, n_rows, diagonal, row_ok=None):
        krows = pl.ds(pl.multiple_of(j * blk, blk), blk)
        rows = slice(r0, r0 + n_rows)
        pv = []
        for p in range(2):
            z = _dot_nt(q_heads[p][rows, :], k_ref[0, krows, :])
            neg_abs = pltpu.bitcast(pltpu.bitcast(z, jnp.uint32) | jnp.uint32(0x80000000), F32)
            sp = jnp.maximum(z, 0.0) + jnp.log2(1.0 + jnp.exp2(neg_abs))
            sp_m = sp
            if diagonal:
                sp_m = keep(causal, sp_m, blk)
            if row_ok is not None:
                sp_m = keep(row_ok, sp_m, None)
            hi, lo = _split_bf16(sp_m)
            cs = _dot(jnp.concatenate([hi, lo], axis=1), suffix)
            a = jnp.exp2(z - sp - cs[:, :blk] - carry_scr[p, rows, :])
            if diagonal:
                a = keep(causal, a, blk)
            if row_ok is not None:
                a = keep(row_ok, a, None)
            pv.append(_dot(a.astype(BF16), vm_scr[p, krows, :]))
            carry_scr[p, rows, :] += cs[:, blk:]
        acc_scr[rows, :] += pv[0] + pv[1]

    base = qt * sub_blocks
    for jj in reversed(range(sub_blocks)):
        sweep(base + jj, jj * blk, min(SB_BAND, sub_blocks - jj) * blk, True)

    @pl.when(qt > 0)
    def _():
        for dist in range(1, SB_BAND):
            sweep(base - dist, 0, (SB_BAND - dist) * blk, False)

    row_id = lax.broadcasted_iota(jnp.int32, (qt_rows, 1), 0)
    has_far_keys = row_id + base * blk >= SB_BAND * blk

    def least_carry():
        c = jnp.minimum(carry_scr[0], carry_scr[1])
        return jnp.min(jnp.where(has_far_keys, c, SB_ZERO_LOG2))

    def far_cond(state):
        j, least = state
        return jnp.logical_and(j >= 0, least < SB_ZERO_LOG2)

    def far_body(state):
        j, _ = state
        sweep(j, 0, qt_rows, False, row_id >= (j - base + SB_BAND) * blk)
        return j - 1, least_carry()

    lax.while_loop(far_cond, far_body, (base + sub_blocks - 1 - SB_BAND, least_carry()))
    o_ref[0] = acc_scr[...].astype(o_ref.dtype)


def stickbreak_core(qkv):
    b, t, _ = qkv.shape
    pairs = SB_HEADS * SB_HEAD_DIM // LANES
    qt_rows = min(SB_QTILE, t)
    assert 2 * SB_HEAD_DIM == LANES and t % qt_rows == 0 and qt_rows % SB_BLOCK == 0
    return pl.pallas_call(
        _sb_kernel,
        out_shape=jax.ShapeDtypeStruct((b, t, pairs * LANES), BF16),
        grid=(b, pairs, t // qt_rows),
        in_specs=[
            pl.BlockSpec((1, qt_rows, LANES), lambda bi, hp, qi: (bi, qi, hp)),
            pl.BlockSpec((1, t, LANES), lambda bi, hp, qi: (bi, 0, pairs + hp)),
            pl.BlockSpec((1, t, LANES), lambda bi, hp, qi: (bi, 0, 2 * pairs + hp)),
        ],
        out_specs=pl.BlockSpec((1, qt_rows, LANES), lambda bi, hp, qi: (bi, qi, hp)),
        scratch_shapes=[
            pltpu.VMEM((2, t, LANES), BF16),
            pltpu.VMEM((qt_rows, LANES), F32), pltpu.VMEM((2, qt_rows, LANES), F32),
        ],
        compiler_params=_params("parallel", "parallel", "arbitrary"),
        name="stickbreak_core",
    )(qkv, qkv, qkv)


def _pick(n, candidates):
    for cand in candidates:
        if n % cand == 0:
            return cand
    raise ValueError(f"no tile for {n}")


def kernel(x, c, ada_w, ada_b, norm1_g, norm2_g, dn_w_in, dn_conv_w, dn_a_log, dn_dt_bias, dn_onorm_g, dn_w_out,
           sb_w_qkv, sb_q_norm_g, sb_k_norm_g, sb_w_out, ffn_w_in, ffn_w_out):
    b, t, d = x.shape
    depth = ada_w.shape[0]
    m = b * t
    tm_in = _pick(t, (512, 256, 128))
    tm_out = _pick(t, (1024, 512, 256, 128))

    mod = adaln_modulation(c, ada_w, ada_b, tn=_pick(N_MOD * d, (2048, 1024, 512, 128)))
    mod3 = mod.reshape(depth * b, 1, N_MOD * d)
    g1 = norm1_g.reshape(depth, 1, d)
    g2 = norm2_g.reshape(depth, 1, d)

    dn_main = 4 * DN_HEADS * DN_HEAD_DIM
    dn_w_in16 = dn_w_in.astype(BF16)
    dn_w_ab16 = jnp.pad(dn_w_in[:, :, dn_main:], ((0, 0), (0, 0), (0, LANES - 2 * DN_HEADS))).astype(BF16)
    dn_w_out16 = dn_w_out.astype(BF16)
    sb_w_qkv16 = sb_w_qkv.astype(BF16)
    sb_w_out16 = sb_w_out.astype(BF16)
    ffn_w_in16 = ffn_w_in.astype(BF16)
    ffn_w_out16 = ffn_w_out.astype(BF16)

    x2 = x.reshape(m, d)
    for i in range(depth):
        j = i // 2
        if i % 2 == 0:
            proj, ab = norm_mod_matmul(x2, g1, mod3, i, dn_w_in16, j, dn_main, w_side=dn_w_ab16, rows_per_batch=t,
                                       tm=tm_in, out_dtype=BF16)
            y = deltanet_core(proj.reshape(b, t, dn_main), ab.reshape(b, t, LANES), dn_conv_w[j], dn_a_log[j],
                              dn_dt_bias[j], dn_onorm_g[j])
            w_out = dn_w_out16
        else:
            n_in = sb_w_qkv16.shape[-1]
            head_gain = jnp.concatenate([jnp.tile(sb_q_norm_g[j] * (SB_HEAD_DIM ** -0.5 * LOG2_E), SB_HEADS),
                                         jnp.tile(sb_k_norm_g[j], SB_HEADS)]).reshape(1, -1)
            qkv = norm_mod_matmul(x2, g1, mod3, i, sb_w_qkv16, j, n_in, head_gain=head_gain, rows_per_batch=t,
                                  tm=tm_in, out_dtype=BF16)
            y = stickbreak_core(qkv.reshape(b, t, n_in))
            w_out = sb_w_out16
        x2 = matmul_gate_residual(y.reshape(m, d), w_out, j, x2, mod3, i, rows_per_batch=t, tm=tm_out)
        x2 = ffn_sublayer(x2, g2, mod3, i, ffn_w_in16, ffn_w_out16, rows_per_batch=t, tm=_pick(t, (256, 128)))
    return x2.reshape(b, t, d)
```
